```python
import jax
import jax.numpy as jnp
from jax import lax
import numpy as np

D_MODEL = 2048
BATCH = 2
SEQ = 8192
DEPTH = 4

GRID_W = 64
CTX_LEN = 256
N_MIXERS = 3
EPS = 1e-6
D_FF = 4 * D_MODEL
N_MOD = 6

NA_HEADS = 16
NA_HEAD_DIM = D_MODEL // NA_HEADS
NA_WIN_R = 8
NA_WIN_C = 16

MLA_HEADS = 16
MLA_Q_RANK = 512
MLA_KV_RANK = 512
MLA_NOPE = 128
MLA_ROPE = 64
MLA_V = D_MODEL // MLA_HEADS
ROPE_PAIRS = MLA_ROPE // 4
ROPE_BASE = 10000.0
Q_BLOCK = 128

ML_HEADS = 8
ML_V = D_MODEL // ML_HEADS
ML_QK = ML_V // 2
ML_CHUNK = 64
ML_CONV = 3

kernel_name = 'hybrid_na_mla_mlstm_diffusion_trunk'

F32 = jnp.float32


def rms_norm(x, g):
    xf = x.astype(F32)
    y = xf * lax.rsqrt(jnp.mean(xf * xf, axis=-1, keepdims=True) + EPS)
    return (y * g.astype(F32)).astype(x.dtype)


def modulate(x, shift, scale):
    return x * (1 + scale) + shift


def sq_relu_mlp(a, w1, w2):
    return jnp.square(jax.nn.relu(a @ w1)) @ w2


def dense_attn(q, k, v):
    s = jnp.einsum('bqhd,bkhd->bhqk', q, k, preferred_element_type=F32)
    p = jax.nn.softmax(s, axis=-1).astype(v.dtype)
    return jnp.einsum('bhqk,bkhd->bqhd', p, v)


def axial_rope_angles(length):
    t = jnp.arange(length)
    pos = jnp.stack([t // GRID_W, t % GRID_W], axis=-1).astype(F32)
    freqs = ROPE_BASE ** (-jnp.arange(ROPE_PAIRS, dtype=F32) / ROPE_PAIRS)
    return pos[:, :, None] * freqs


def apply_axial_rope(x, ang):
    xs = x.astype(F32).reshape(*x.shape[:-1], 2, 2, ROPE_PAIRS)
    x1, x2 = xs[..., 0, :], xs[..., 1, :]
    cos, sin = jnp.cos(ang), jnp.sin(ang)
    out = jnp.stack([x1 * cos - x2 * sin, x1 * sin + x2 * cos], axis=-2)
    return out.reshape(x.shape).astype(x.dtype)


def na_mixer(a_ctx, a_lat, w_qkv, rpb, w_o, need_ctx):
    B, S, D = a_lat.shape
    Lc = a_ctx.shape[1]
    rows = S // GRID_W
    kr = min(NA_WIN_R, rows)
    H, dh = NA_HEADS, NA_HEAD_DIM
    scale = dh ** -0.5
    qkv = (a_lat @ w_qkv).reshape(B, rows, GRID_W, 3, H, dh)
    q, k, v = qkv[:, :, :, 0] * scale, qkv[:, :, :, 1], qkv[:, :, :, 2]
    qkv_c = (a_ctx @ w_qkv).reshape(B, Lc, 3, H, dh)
    q_c, k_c, v_c = qkv_c[:, :, 0] * scale, qkv_c[:, :, 1], qkv_c[:, :, 2]

    qcol = np.arange(GRID_W)
    col_start = np.clip(qcol - NA_WIN_C // 2, 0, GRID_W - NA_WIN_C)
    col_mask = (qcol[None, :] >= col_start[:, None]) & (qcol[None, :] < col_start[:, None] + NA_WIN_C)
    dc_idx = np.clip(qcol[None, :] - qcol[:, None], -(NA_WIN_C - 1), NA_WIN_C - 1) + NA_WIN_C - 1
    rpb_cols = rpb[:, :, dc_idx]
    n_win = kr * GRID_W

    def row_block(r):
        rs = jnp.clip(r - kr // 2, 0, rows - kr)
        q_r = lax.dynamic_index_in_dim(q, r, axis=1, keepdims=False)
        k_r = lax.dynamic_slice_in_dim(k, rs, kr, axis=1)
        v_r = lax.dynamic_slice_in_dim(v, rs, kr, axis=1)
        bias = jnp.take(rpb_cols, rs + jnp.arange(kr) - r + NA_WIN_R - 1, axis=1)
        s_win = jnp.einsum('bqhd,brkhd->bhqrk', q_r, k_r, preferred_element_type=F32) + bias.transpose(0, 2, 1, 3).astype(F32)
        s_win = jnp.where(col_mask[:, None, :], s_win, -jnp.inf)
        s_ctx = jnp.einsum('bqhd,bnhd->bhqn', q_r, k_c, preferred_element_type=F32)
        p = jax.nn.softmax(jnp.concatenate([s_win.reshape(B, H, GRID_W, n_win), s_ctx], axis=-1), axis=-1)
        p_win = p[..., :n_win].reshape(B, H, GRID_W, kr, GRID_W).astype(v.dtype)
        p_ctx = p[..., n_win:].astype(v.dtype)
        return jnp.einsum('bhqrk,brkhd->bqhd', p_win, v_r) + jnp.einsum('bhqn,bnhd->bqhd', p_ctx, v_c)

    o = lax.map(row_block, jnp.arange(rows))
    y_lat = jnp.moveaxis(o, 0, 1).reshape(B, S, D) @ w_o
    y_ctx = dense_attn(q_c, k_c, v_c).reshape(B, Lc, D) @ w_o if need_ctx else None
    return y_ctx, y_lat


def mla_mixer(a_ctx, a_lat, w_in, g_q, g_kv, w_uq, w_ukv, w_o, need_ctx):
    B, S, D = a_lat.shape
    Lc = a_ctx.shape[1]
    H = MLA_HEADS
    dqk = MLA_NOPE + MLA_ROPE
    scale = dqk ** -0.5

    def project(a, ang):
        L = a.shape[1]
        z = a @ w_in
        cq, ckv, k_pe = jnp.split(z, [MLA_Q_RANK, MLA_Q_RANK + MLA_KV_RANK], axis=-1)
        qh = (rms_norm(cq, g_q) @ w_uq).reshape(B, L, H, dqk)
        kv = (rms_norm(ckv, g_kv) @ w_ukv).reshape(B, L, H, MLA_NOPE + MLA_V)
        q_nope, q_pe = qh[..., :MLA_NOPE], qh[..., MLA_NOPE:]
        if ang is not None:
            q_pe = apply_axial_rope(q_pe, ang[:, None])
            k_pe = apply_axial_rope(k_pe, ang)
        qh = jnp.concatenate([q_nope, q_pe], axis=-1) * scale
        kh = jnp.concatenate([kv[..., :MLA_NOPE], jnp.broadcast_to(k_pe[:, :, None, :], (B, L, H, MLA_ROPE))], axis=-1)
        return qh, kh, kv[..., MLA_NOPE:]

    q_l, k_l, v_l = project(a_lat, axial_rope_angles(S))
    q_c, k_c, v_c = project(a_ctx, None)
    k_all = jnp.concatenate([k_c, k_l], axis=1).transpose(0, 2, 1, 3)
    v_all = jnp.concatenate([v_c, v_l], axis=1).transpose(0, 2, 1, 3)
    qb = q_l.reshape(B, S // Q_BLOCK, Q_BLOCK, H, dqk).transpose(1, 0, 3, 2, 4)

    def block(q_blk):
        s = jnp.einsum('bhqd,bhkd->bhqk', q_blk, k_all, preferred_element_type=F32)
        p = jax.nn.softmax(s, axis=-1).astype(v_all.dtype)
        return jnp.einsum('bhqk,bhkd->bqhd', p, v_all)

    o = lax.map(block, qb)
    y_lat = jnp.moveaxis(o, 0, 1).reshape(B, S, H * MLA_V) @ w_o
    y_ctx = dense_attn(q_c, k_c, v_c).reshape(B, Lc, H * MLA_V) @ w_o if need_ctx else None
    return y_ctx, y_lat


def depthwise_conv_centred(x, w):
    pad = ML_CONV // 2
    L = x.shape[1]
    xp = jnp.pad(x, ((0, 0), (pad, pad), (0, 0)))
    return sum(xp[:, j:j + L] * w[j] for j in range(ML_CONV))


def mlstm_zero_state(B):
    return (jnp.zeros((B, ML_HEADS, ML_QK, ML_V), F32), jnp.zeros((B, ML_HEADS, ML_QK), F32), jnp.zeros((B, ML_HEADS), F32))


def mlstm_chunk_scan(q, k, v, li, lf, state):
    B, H, L, dk = q.shape
    dv = v.shape[-1]
    n_chunks = L // ML_CHUNK
    tril = jnp.tril(jnp.ones((ML_CHUNK, ML_CHUNK), dtype=bool))

    def chunks(t):
        return jnp.moveaxis(t.reshape(B, H, n_chunks, ML_CHUNK, *t.shape[3:]), 2, 0)

    def step(carry, xs):
        C, n, m = carry
        qc, kc, vc, lic, lfc = xs
        b = jnp.cumsum(lfc, axis=-1)
        log_w = jnp.where(tril, b[..., :, None] - b[..., None, :] + lic[..., None, :], -jnp.inf)
        log_inter = b + m[..., None]
        m_t = jnp.maximum(log_inter, jnp.max(log_w, axis=-1))
        w_inter = jnp.exp(log_inter - m_t)
        s = jnp.einsum('bhtd,bhsd->bhts', qc, kc) * jnp.exp(log_w - m_t[..., None])
        num = w_inter[..., None] * jnp.einsum('bhtd,bhde->bhte', qc, C) + jnp.einsum('bhts,bhse->bhte', s, vc)
        den = w_inter * jnp.einsum('bhtd,bhd->bht', qc, n) + jnp.sum(s, axis=-1)
        h = num / jnp.maximum(jnp.abs(den), jnp.exp(-m_t))[..., None]
        b_end = b[..., -1]
        log_s = b_end[..., None] - b + lic
        m_new = jnp.maximum(b_end + m, jnp.max(log_s, axis=-1))
        decay = jnp.exp(b_end + m - m_new)
        kw = kc * jnp.exp(log_s - m_new[..., None])[..., None]
        C_new = decay[..., None, None] * C + jnp.einsum('bhsd,bhse->bhde', kw, vc)
        n_new = decay[..., None] * n + jnp.sum(kw, axis=2)
        return (C_new, n_new, m_new), h

    state, h = lax.scan(step, state, (chunks(q), chunks(k), chunks(v), chunks(li), chunks(lf)))
    return state, jnp.moveaxis(h, 0, 2).reshape(B, H, L, dv)


def mlstm_mixer(a_ctx, a_lat, w_in, b_gate, conv_w, g_head, w_o, need_ctx):
    H, dk, dv = ML_HEADS, ML_QK, ML_V
    HK = H * dk

    def project(a):
        B, L, _ = a.shape
        z = a @ w_in
        qk, v, o, g = jnp.split(z, [2 * HK, 2 * HK + H * dv, 2 * HK + 2 * H * dv], axis=-1)
        qk = jax.nn.silu(depthwise_conv_centred(qk, conv_w))
        heads = lambda t, d: t.reshape(B, L, H, d).transpose(0, 2, 1, 3).astype(F32)
        g = (g.astype(F32) + b_gate.astype(F32)).reshape(B, L, 4, H).transpose(2, 0, 3, 1)
        return heads(qk[..., :HK], dk) * (dk ** -0.5), heads(qk[..., HK:], dk), heads(v, dv), o, g

    def bidir(q, k, v, g, init_f, init_b):
        st_f, h_f = mlstm_chunk_scan(q, k, v, g[0], jax.nn.log_sigmoid(g[1]), init_f)
        fl = lambda t: jnp.flip(t, axis=2)
        st_b, h_b = mlstm_chunk_scan(fl(q), fl(k), fl(v), jnp.flip(g[2], -1), jax.nn.log_sigmoid(jnp.flip(g[3], -1)), init_b)
        return h_f + fl(h_b), st_f, st_b

    def output(h, o):
        B, _, L, _ = h.shape
        hn = rms_norm(h, g_head.reshape(H, 1, dv)).transpose(0, 2, 1, 3).reshape(B, L, H * dv).astype(o.dtype)
        return (jax.nn.sigmoid(o) * hn) @ w_o

    qc, kc, vc, oc, gc = project(a_ctx)
    ql, kl, vl, ol, gl = project(a_lat)
    zero = mlstm_zero_state(a_lat.shape[0])
    h_c, st_f, st_b = bidir(qc, kc, vc, gc, zero, zero)
    h_l, _, _ = bidir(ql, kl, vl, gl, st_f, st_b)
    y_ctx = output(h_c, oc) if need_ctx else None
    return y_ctx, output(h_l, ol)


def setup_inputs(seed: int = 0) -> dict:
    key = jax.random.key(seed)
    ks = jax.random.split(key, 32)
    D = D_MODEL
    n_na = (DEPTH + N_MIXERS - 1) // N_MIXERS
    n_mla = (DEPTH + N_MIXERS - 2) // N_MIXERS
    n_ml = (DEPTH + N_MIXERS - 3) // N_MIXERS

    def nrm(k, shape, std):
        return std * jax.random.normal(k, shape, F32)

    def lin(k, shape):
        return nrm(k, shape, shape[-2] ** -0.5)

    def gain(k, shape):
        return 1.0 + nrm(k, shape, 0.02)

    mla_in = MLA_Q_RANK + MLA_KV_RANK + MLA_ROPE
    ml_in = 2 * ML_HEADS * ML_QK + 2 * ML_HEADS * ML_V + 4 * ML_HEADS
    i_bias = nrm(ks[20], (n_ml, 2, ML_HEADS), 0.1)
    f_bias = 3.0 + 3.0 * jax.random.uniform(ks[21], (n_ml, 2, ML_HEADS), F32)
    ml_b_gate = jnp.stack([i_bias[:, 0], f_bias[:, 0], i_bias[:, 1], f_bias[:, 1]], axis=1).reshape(n_ml, 4 * ML_HEADS)
    return {
        'x': nrm(ks[0], (BATCH, SEQ, D), 1.0),
        'c': nrm(ks[1], (BATCH, D), 1.0),
        'ctx': nrm(ks[2], (BATCH, CTX_LEN, D), 1.0),
        'c_ctx': nrm(ks[3], (D,), 1.0),
        'ada_w': nrm(ks[4], (DEPTH, D, N_MOD * D), 0.5 * D ** -0.5),
        'ada_b': nrm(ks[5], (DEPTH, N_MOD * D), 0.02),
        'norm_g': gain(ks[6], (DEPTH, 4, D)),
        'ff_w1': lin(ks[7], (DEPTH, D, D_FF)),
        'ff_w2': lin(ks[8], (DEPTH, D_FF, D)),
        'na_w_qkv': lin(ks[9], (n_na, D, 3 * NA_HEADS * NA_HEAD_DIM)),
        'na_rpb': nrm(ks[10], (n_na, NA_HEADS, 2 * NA_WIN_R - 1, 2 * NA_WIN_C - 1), 0.1),
        'na_w_o': lin(ks[11], (n_na, NA_HEADS * NA_HEAD_DIM, D)),
        'mla_w_in': lin(ks[12], (n_mla, D, mla_in)),
        'mla_g_q': gain(ks[13], (n_mla, MLA_Q_RANK)),
        'mla_g_kv': gain(ks[14], (n_mla, MLA_KV_RANK)),
        'mla_w_uq': lin(ks[15], (n_mla, MLA_Q_RANK, MLA_HEADS * (MLA_NOPE + MLA_ROPE))),
        'mla_w_ukv': lin(ks[16], (n_mla, MLA_KV_RANK, MLA_HEADS * (MLA_NOPE + MLA_V))),
        'mla_w_o': lin(ks[17], (n_mla, MLA_HEADS * MLA_V, D)),
        'ml_w_in': lin(ks[18], (n_ml, D, ml_in)),
        'ml_b_gate': ml_b_gate,
        'ml_conv': nrm(ks[22], (n_ml, ML_CONV, 2 * ML_HEADS * ML_QK), ML_CONV ** -0.5),
        'ml_g_head': gain(ks[23], (n_ml, ML_HEADS * ML_V)),
        'ml_w_o': lin(ks[24], (n_ml, ML_HEADS * ML_V, D)),
    }


def reference(x, c, ctx, c_ctx, ada_w, ada_b, norm_g, ff_w1, ff_w2, na_w_qkv, na_rpb, na_w_o, mla_w_in, mla_g_q, mla_g_kv, mla_w_uq, mla_w_ukv, mla_w_o, ml_w_in, ml_b_gate, ml_conv, ml_g_head, ml_w_o):
    h_lat, h_ctx = x, ctx
    s_c, s_cc = jax.nn.silu(c), jax.nn.silu(c_ctx)
    for i in range(DEPTH):
        last = i == DEPTH - 1
        sh1, sc1, g1, sh2, sc2, g2 = jnp.split((s_c @ ada_w[i] + ada_b[i])[:, None, :], N_MOD, axis=-1)
        sh1c, sc1c, g1c, sh2c, sc2c, g2c = jnp.split(s_cc @ ada_w[i] + ada_b[i], N_MOD, axis=-1)
        g_pre1, g_post1, g_pre2, g_post2 = norm_g[i]
        a_l = modulate(rms_norm(h_lat, g_pre1), sh1, sc1)
        a_c = modulate(rms_norm(h_ctx, g_pre1), sh1c, sc1c)
        kind, j = i % N_MIXERS, i // N_MIXERS
        if kind == 0:
            y_c, y_l = na_mixer(a_c, a_l, na_w_qkv[j], na_rpb[j], na_w_o[j], not last)
        elif kind == 1:
            y_c, y_l = mla_mixer(a_c, a_l, mla_w_in[j], mla_g_q[j], mla_g_kv[j], mla_w_uq[j], mla_w_ukv[j], mla_w_o[j], not last)
        else:
            y_c, y_l = mlstm_mixer(a_c, a_l, ml_w_in[j], ml_b_gate[j], ml_conv[j], ml_g_head[j], ml_w_o[j], not last)
        h_lat = h_lat + g1 * rms_norm(y_l, g_post1)
        h_lat = h_lat + g2 * rms_norm(sq_relu_mlp(modulate(rms_norm(h_lat, g_pre2), sh2, sc2), ff_w1[i], ff_w2[i]), g_post2)
        if not last:
            h_ctx = h_ctx + g1c * rms_norm(y_c, g_post1)
            h_ctx = h_ctx + g2c * rms_norm(sq_relu_mlp(modulate(rms_norm(h_ctx, g_pre2), sh2c, sc2c), ff_w1[i], ff_w2[i]), g_post2)
    return h_lat
```

```python
import functools
import math

import numpy as np
import jax
import jax.numpy as jnp
from jax import lax
from jax.experimental import pallas as pl
from jax.experimental.pallas import tpu as pltpu

F32 = jnp.float32
BF16 = jnp.bfloat16

EPS = 1e-6
N_MOD = 6
GRID_W = 64

NA_HEADS = 16
NA_WIN_R = 8
NA_WIN_C = 16
NA_Q_ROWS = 4
NA_K_ROWS = 12

MLA_HEADS = 16
MLA_Q_RANK = 512
MLA_KV_RANK = 512
MLA_NOPE = 128
MLA_ROPE = 64
MLA_V = 128
ROPE_PAIRS = MLA_ROPE // 4
ROPE_BASE = 10000.0
MLA_QK_PAD = 256

ML_HEADS = 8
ML_QK = 128
ML_V = 256
ML_CONV = 3
ML_CHUNK = 256

LANE = 128
NEG_BIG = -1e30
VMEM_LIMIT_BYTES = 56 * 1024 * 1024


def _cparams(*sem):
    return pltpu.CompilerParams(dimension_semantics=sem, vmem_limit_bytes=VMEM_LIMIT_BYTES)


def _rms(x, g):
    return x * lax.rsqrt(jnp.mean(x * x, axis=-1, keepdims=True) + EPS) * g


def _dot(a, b):
    return jnp.dot(a, b, preferred_element_type=F32)


def _dot_nt(a, b):
    return lax.dot_general(a, b, (((1,), (1,)), ((), ())), preferred_element_type=F32)


def _dot_tn(a, b):
    return lax.dot_general(a, b, (((0,), (0,)), ((), ())), preferred_element_type=F32)


def _mod_index_map(tm, rows_per_mod, mod_base):
    return lambda i, *_: (mod_base + (i * tm) // rows_per_mod, 0, 0)


def _ada_kernel(c_ref, w_ref, b_ref, o_ref):
    c = c_ref[...]
    s = c * jax.nn.sigmoid(c)
    o_ref[...] = _dot(s, w_ref[...]) + b_ref[...]


def ada_modulation(c_all, ada_w, ada_b, *, tn=1024):
    depth, d, n = ada_w.shape
    r = c_all.shape[0]
    return pl.pallas_call(
        _ada_kernel,
        grid=(depth, n // tn),
        in_specs=[
            pl.BlockSpec((r, d), lambda l, j: (0, 0)),
            pl.BlockSpec((None, d, tn), lambda l, j: (l, 0, j)),
            pl.BlockSpec((None, 1, tn), lambda l, j: (l, 0, j)),
        ],
        out_specs=pl.BlockSpec((None, r, tn), lambda l, j: (l, 0, j)),
        out_shape=jax.ShapeDtypeStruct((depth, r, n), F32),
        compiler_params=_cparams("arbitrary", "arbitrary"),
        name="ada_modulation",
    )(c_all, ada_w, ada_b.reshape(depth, 1, n))


def _norm_proj_kernel(*refs, shift_idx, scale_idx):
    if shift_idx is None:
        x_ref, g_ref, w_ref, o_ref, a_scr = refs
    else:
        x_ref, g_ref, mod_ref, w_ref, o_ref, a_scr = refs

    @pl.when(pl.program_id(1) == 0)
    def _():
        y = _rms(x_ref[...], g_ref[...])
        if shift_idx is not None:
            y = y * (1.0 + mod_ref[scale_idx:scale_idx + 1, :]) + mod_ref[shift_idx:shift_idx + 1, :]
        a_scr[...] = y.astype(BF16)

    o_ref[...] = _dot(a_scr[...], w_ref[...]).astype(o_ref.dtype)


def norm_proj(x, g, w, *, out_dtype, tm, tn, mod=None, rows_per_mod=None, mod_base=0,
              shift_idx=None, scale_idx=None, x_col_block=0):
    rows = x.shape[0]
    k, n = w.shape
    tm = min(tm, rows)
    tn = min(tn, n)
    in_specs = [
        pl.BlockSpec((tm, k), lambda i, j: (i, x_col_block)),
        pl.BlockSpec((1, k), lambda i, j: (0, 0)),
    ]
    args = [x, g.reshape(1, k)]
    if shift_idx is not None:
        in_specs.append(pl.BlockSpec((None, N_MOD, k), _mod_index_map(tm, rows_per_mod, mod_base)))
        args.append(mod)
    in_specs.append(pl.BlockSpec((k, tn), lambda i, j: (0, j)))
    args.append(w)
    return pl.pallas_call(
        functools.partial(_norm_proj_kernel, shift_idx=shift_idx, scale_idx=scale_idx),
        grid=(rows // tm, n // tn),
        in_specs=in_specs,
        out_specs=pl.BlockSpec((tm, tn), lambda i, j: (i, j)),
        out_shape=jax.ShapeDtypeStruct((rows, n), out_dtype),
        scratch_shapes=[pltpu.VMEM((tm, k), BF16)],
        compiler_params=_cparams("parallel", "arbitrary"),
        name="norm_proj",
    )(*args)


def _out_proj_kernel(o_ref, w_ref, h_ref, g_ref, mod_ref, out_ref, *, gate_idx):
    y = _dot(o_ref[...], w_ref[...])
    out_ref[...] = h_ref[...] + mod_ref[gate_idx:gate_idx + 1, :] * _rms(y, g_ref[...])


def out_proj_residual(o, w, h, g, mod, *, gate_idx, rows_per_mod, mod_base, tm=512):
    rows, k = o.shape
    d = w.shape[1]
    tm = min(tm, rows)
    return pl.pallas_call(
        functools.partial(_out_proj_kernel, gate_idx=gate_idx),
        grid=(rows // tm,),
        in_specs=[
            pl.BlockSpec((tm, k), lambda i: (i, 0)),
            pl.BlockSpec((k, d), lambda i: (0, 0)),
            pl.BlockSpec((tm, d), lambda i: (i, 0)),
            pl.BlockSpec((1, d), lambda i: (0, 0)),
            pl.BlockSpec((None, N_MOD, d), _mod_index_map(tm, rows_per_mod, mod_base)),
        ],
        out_specs=pl.BlockSpec((tm, d), lambda i: (i, 0)),
        out_shape=jax.ShapeDtypeStruct((rows, d), F32),
        compiler_params=_cparams("parallel"),
        name="out_proj_residual",
    )(o, w, h, g.reshape(1, d), mod)


def _ffn_kernel(x_ref, g1_ref, mod_ref, w1_ref, w2_ref, g2_ref, out_ref, a_scr, acc_scr,
                *, shift_idx, scale_idx, gate_idx):
    f = pl.program_id(1)

    @pl.when(f == 0)
    def _():
        y = _rms(x_ref[...], g1_ref[...])
        y = y * (1.0 + mod_ref[scale_idx:scale_idx + 1, :]) + mod_ref[shift_idx:shift_idx + 1, :]
        a_scr[...] = y.astype(BF16)
        acc_scr[...] = jnp.zeros_like(acc_scr)

    hid = jnp.maximum(_dot(a_scr[...], w1_ref[...]), 0.0)
    acc_scr[...] += _dot((hid * hid).astype(BF16), w2_ref[...])

    @pl.when(f == pl.num_programs(1) - 1)
    def _():
        out_ref[...] = x_ref[...] + mod_ref[gate_idx:gate_idx + 1, :] * _rms(acc_scr[...], g2_ref[...])


def ffn_residual(x, g_pre, g_post, mod, w1, w2, *, rows_per_mod, mod_base, tm=512, tf=512):
    rows, d = x.shape
    dff = w1.shape[1]
    tm = min(tm, rows)
    return pl.pallas_call(
        functools.partial(_ffn_kernel, shift_idx=3, scale_idx=4, gate_idx=5),
        grid=(rows // tm, dff // tf),
        in_specs=[
            pl.BlockSpec((tm, d), lambda i, f: (i, 0)),
            pl.BlockSpec((1, d), lambda i, f: (0, 0)),
            pl.BlockSpec((None, N_MOD, d), _mod_index_map(tm, rows_per_mod, mod_base)),
            pl.BlockSpec((d, tf), lambda i, f: (0, f)),
            pl.BlockSpec((tf, d), lambda i, f: (f, 0)),
            pl.BlockSpec((1, d), lambda i, f: (0, 0)),
        ],
        out_specs=pl.BlockSpec((tm, d), lambda i, f: (i, 0)),
        out_shape=jax.ShapeDtypeStruct((rows, d), F32),
        scratch_shapes=[pltpu.VMEM((tm, d), BF16), pltpu.VMEM((tm, d), F32)],
        compiler_params=_cparams("parallel", "arbitrary"),
        name="ffn_residual",
    )(x, g_pre.reshape(1, d), mod, w1, w2, g_post.reshape(1, d))


def _ctx_attn_kernel(q_ref, k_ref, v_ref, o_ref, *, scale):
    s = _dot_nt(q_ref[...], k_ref[...]) * scale
    m = jnp.max(s, axis=-1, keepdims=True)
    p = jnp.exp(s - m)
    l = jnp.sum(p, axis=-1, keepdims=True)
    o_ref[...] = (_dot(p.astype(BF16), v_ref[...]) / l).astype(o_ref.dtype)


def ctx_attention(q_arr, k_arr, v_arr, *, batch, heads, lc, dq, dv, q_off, k_off, v_off, scale):
    return pl.pallas_call(
        functools.partial(_ctx_attn_kernel, scale=scale),
        grid=(batch, heads),
        in_specs=[
            pl.BlockSpec((lc, dq), lambda b, h: (b, q_off + h)),
            pl.BlockSpec((lc, dq), lambda b, h: (b, k_off + h)),
            pl.BlockSpec((lc, dv), lambda b, h: (b, v_off + h)),
        ],
        out_specs=pl.BlockSpec((lc, dv), lambda b, h: (b, h)),
        out_shape=jax.ShapeDtypeStruct((batch * lc, heads * dv), BF16),
        compiler_params=_cparams("parallel", "parallel"),
        name="ctx_attention",
    )(q_arr, k_arr, v_arr)


def _na_window_tables(img_rows):
    kr = min(NA_WIN_R, img_rows)
    n_blk = img_rows // NA_Q_ROWS

    def tables(i):
        ks = int(np.clip(i * NA_Q_ROWS - NA_WIN_R // 2, 0, img_rows - NA_K_ROWS))
        r = i * NA_Q_ROWS + np.arange(NA_Q_ROWS)[:, None]
        kr_abs = ks + np.arange(NA_K_ROWS)[None, :]
        rs = np.clip(r - kr // 2, 0, img_rows - kr)
        valid = (kr_abs >= rs) & (kr_abs < rs + kr)
        dr = np.clip(kr_abs - r + NA_WIN_R - 1, 0, 2 * NA_WIN_R - 2)
        return ks, valid, dr

    first, inner, last = tables(0), tables(1), tables(n_blk - 1)
    for i in range(1, n_blk - 1):
        ks, valid, dr = tables(i)
        assert ks == i * NA_Q_ROWS - NA_WIN_R // 2
        assert (valid == inner[1]).all() and (dr[valid] == inner[2][valid]).all()
    assert first[0] == 0 and last[0] == img_rows - NA_K_ROWS
    return [first, inner, last]


def _na_bias_tables(rpb, img_rows):
    qcol = np.arange(GRID_W)
    col_start = np.clip(qcol - NA_WIN_C // 2, 0, GRID_W - NA_WIN_C)
    col_mask = (qcol[None, :] >= col_start[:, None]) & (qcol[None, :] < col_start[:, None] + NA_WIN_C)
    dc_idx = np.clip(qcol[None, :] - qcol[:, None], -(NA_WIN_C - 1), NA_WIN_C - 1) + NA_WIN_C - 1
    rpb_cols = rpb[:, :, dc_idx]
    out = []
    for _, valid, dr in _na_window_tables(img_rows):
        t = rpb_cols[:, dr]
        mask = valid[:, :, None, None] & col_mask[None, None]
        t = jnp.where(mask[None], t, NEG_BIG)
        t = t.transpose(0, 1, 3, 2, 4).reshape(rpb.shape[0], NA_Q_ROWS * GRID_W, NA_K_ROWS * GRID_W)
        out.append(t)
    return jnp.stack(out, axis=1)


def _na_kernel(q_ref, k_ref, v_ref, kc_ref, vc_ref, bias_ref, o_ref, *, img_rows, scale):
    qb = NA_Q_ROWS * GRID_W
    kb = NA_K_ROWS * GRID_W
    n_blk = img_rows // NA_Q_ROWS
    kc = kc_ref[...]
    vc = vc_ref[...]

    def block(i, variant):
        if isinstance(i, int):
            q0 = i * qb
            ks = min(max(i * NA_Q_ROWS - NA_WIN_R // 2, 0), img_rows - NA_K_ROWS) * GRID_W
        else:
            q0 = pl.multiple_of(i * qb, qb)
            ks = pl.multiple_of((i * NA_Q_ROWS - NA_WIN_R // 2) * GRID_W, GRID_W)
        q = q_ref[pl.ds(q0, qb), :]
        s_w = _dot_nt(q, k_ref[pl.ds(ks, kb), :]) * scale + bias_ref[variant]
        s_c = _dot_nt(q, kc) * scale
        m = jnp.maximum(jnp.max(s_w, axis=-1, keepdims=True), jnp.max(s_c, axis=-1, keepdims=True))
        p_w = jnp.exp(s_w - m)
        p_c = jnp.exp(s_c - m)
        l = jnp.sum(p_w, axis=-1, keepdims=True) + jnp.sum(p_c, axis=-1, keepdims=True)
        o = _dot(p_w.astype(BF16), v_ref[pl.ds(ks, kb), :]) + _dot(p_c.astype(BF16), vc)
        o_ref[pl.ds(q0, qb), :] = (o / l).astype(o_ref.dtype)

    block(0, 0)

    def body(i, carry):
        block(i, 1)
        return carry

    lax.fori_loop(1, n_blk - 1, body, 0)
    block(n_blk - 1, 2)


def na_attention(qkv_lat, qkv_ctx, bias, *, batch, seq, lc):
    heads = NA_HEADS
    dh = qkv_lat.shape[1] // (3 * heads)
    img_rows = seq // GRID_W
    qb = NA_Q_ROWS * GRID_W
    kb = NA_K_ROWS * GRID_W
    return pl.pallas_call(
        functools.partial(_na_kernel, img_rows=img_rows, scale=dh ** -0.5),
        grid=(batch, heads),
        in_specs=[
            pl.BlockSpec((seq, dh), lambda b, h: (b, h)),
            pl.BlockSpec((seq, dh), lambda b, h: (b, heads + h)),
            pl.BlockSpec((seq, dh), lambda b, h: (b, 2 * heads + h)),
            pl.BlockSpec((lc, dh), lambda b, h: (b, heads + h)),
            pl.BlockSpec((lc, dh), lambda b, h: (b, 2 * heads + h)),
            pl.BlockSpec((None, 3, qb, kb), lambda b, h: (h, 0, 0, 0)),
        ],
        out_specs=pl.BlockSpec((seq, dh), lambda b, h: (b, h)),
        out_shape=jax.ShapeDtypeStruct((batch * seq, heads * dh), BF16),
        compiler_params=_cparams("parallel", "parallel"),
        name="na_attention",
    )(qkv_lat, qkv_lat, qkv_lat, qkv_ctx, qkv_ctx, bias)


def _mla_up_kernel(z_ref, gq_ref, gkv_ref, wq_ref, wkv_ref, cos_ref, sin_ref,
                   q_ref, k_ref, v_ref, *, scale):
    hq = MLA_HEADS * MLA_QK_PAD
    z = z_ref[...]
    cq = _rms(z[:, :MLA_Q_RANK], gq_ref[...]).astype(BF16)
    ckv = _rms(z[:, MLA_Q_RANK:MLA_Q_RANK + MLA_KV_RANK], gkv_ref[...]).astype(BF16)
    cos = cos_ref[...]
    sin = sin_ref[...]
    c0 = MLA_Q_RANK + MLA_KV_RANK
    kpe = (z[:, c0:c0 + LANE] * cos + z[:, c0 + LANE:c0 + 2 * LANE] * sin).astype(BF16)
    qraw = _dot(cq, wq_ref[...])
    kv = _dot(ckv, wkv_ref[...])
    for h in range(MLA_HEADS):
        a = h * MLA_QK_PAD
        q_ref[:, a:a + LANE] = (qraw[:, a:a + LANE] * scale).astype(BF16)
        pe = qraw[:, a + LANE:a + 2 * LANE]
        sw = qraw[:, hq + h * LANE:hq + (h + 1) * LANE]
        q_ref[:, a + LANE:a + 2 * LANE] = ((pe * cos + sw * sin) * scale).astype(BF16)
        k_ref[:, a:a + LANE] = kv[:, h * MLA_NOPE:(h + 1) * MLA_NOPE].astype(BF16)
        k_ref[:, a + LANE:a + 2 * LANE] = kpe
    v_ref[...] = kv[:, MLA_HEADS * MLA_NOPE:].astype(BF16)


def mla_up(z, g_q, g_kv, wq, wkv, cos, sin, *, tm=256):
    rows = z.shape[0]
    tm = min(tm, rows)
    n_pos = cos.shape[0] // tm
    hq = MLA_HEADS * MLA_QK_PAD
    hv = MLA_HEADS * MLA_V
    return pl.pallas_call(
        functools.partial(_mla_up_kernel, scale=(MLA_NOPE + MLA_ROPE) ** -0.5),
        grid=(rows // tm,),
        in_specs=[
            pl.BlockSpec((tm, z.shape[1]), lambda i: (i, 0)),
            pl.BlockSpec((1, MLA_Q_RANK), lambda i: (0, 0)),
            pl.BlockSpec((1, MLA_KV_RANK), lambda i: (0, 0)),
            pl.BlockSpec(wq.shape, lambda i: (0, 0)),
            pl.BlockSpec(wkv.shape, lambda i: (0, 0)),
            pl.BlockSpec((tm, LANE), lambda i: (i % n_pos, 0)),
            pl.BlockSpec((tm, LANE), lambda i: (i % n_pos, 0)),
        ],
        out_specs=[
            pl.BlockSpec((tm, hq), lambda i: (i, 0)),
            pl.BlockSpec((tm, hq), lambda i: (i, 0)),
            pl.BlockSpec((tm, hv), lambda i: (i, 0)),
        ],
        out_shape=[
            jax.ShapeDtypeStruct((rows, hq), BF16),
            jax.ShapeDtypeStruct((rows, hq), BF16),
            jax.ShapeDtypeStruct((rows, hv), BF16),
        ],
        compiler_params=_cparams("parallel"),
        name="mla_up",
    )(z, g_q.reshape(1, -1), g_kv.reshape(1, -1), wq, wkv, cos, sin)


def _mla_attn_kernel(q_ref, k_ref, v_ref, kc_ref, vc_ref, o_ref, m_scr, l_scr, acc_scr, *, tk):
    q = q_ref[...]
    n_kv = k_ref.shape[0] // tk

    s = _dot_nt(q, kc_ref[...])
    m0 = jnp.max(s, axis=-1, keepdims=True)
    p = jnp.exp(s - m0)
    m_scr[...] = m0
    l_scr[...] = jnp.sum(p, axis=-1, keepdims=True)
    acc_scr[...] = _dot(p.astype(BF16), vc_ref[...])

    def body(c, carry):
        k0 = pl.multiple_of(c * tk, tk)
        s = _dot_nt(q, k_ref[pl.ds(k0, tk), :])
        m_prev = m_scr[...]
        m_new = jnp.maximum(m_prev, jnp.max(s, axis=-1, keepdims=True))
        alpha = jnp.exp(m_prev - m_new)
        p = jnp.exp(s - m_new)
        l_scr[...] = alpha * l_scr[...] + jnp.sum(p, axis=-1, keepdims=True)
        acc_scr[...] = alpha * acc_scr[...] + _dot(p.astype(BF16), v_ref[pl.ds(k0, tk), :])
        m_scr[...] = m_new
        return carry

    lax.fori_loop(0, n_kv, body, 0)
    o_ref[...] = (acc_scr[...] / l_scr[...]).astype(o_ref.dtype)


def mla_attention(q_lat, k_lat, v_lat, k_ctx, v_ctx, *, batch, seq, lc, tq=512, tk=512):
    heads = MLA_HEADS
    tq = min(tq, seq)
    tk = min(tk, seq)
    nq = seq // tq
    return pl.pallas_call(
        functools.partial(_mla_attn_kernel, tk=tk),
        grid=(batch, heads, nq),
        in_specs=[
            pl.BlockSpec((tq, MLA_QK_PAD), lambda b, h, i: (b * nq + i, h)),
            pl.BlockSpec((seq, MLA_QK_PAD), lambda b, h, i: (b, h)),
            pl.BlockSpec((seq, MLA_V), lambda b, h, i: (b, h)),
            pl.BlockSpec((lc, MLA_QK_PAD), lambda b, h, i: (b, h)),
            pl.BlockSpec((lc, MLA_V), lambda b, h, i: (b, h)),
        ],
        out_specs=pl.BlockSpec((tq, MLA_V), lambda b, h, i: (b * nq + i, h)),
        out_shape=jax.ShapeDtypeStruct((batch * seq, heads * MLA_V), BF16),
        scratch_shapes=[pltpu.VMEM((tq, 1), F32), pltpu.VMEM((tq, 1), F32), pltpu.VMEM((tq, MLA_V), F32)],
        compiler_params=_cparams("parallel", "parallel", "arbitrary"),
        name="mla_attention",
    )(q_lat, k_lat, v_lat, k_ctx, v_ctx)


def _rope_tables(seq):
    t = np.arange(seq)
    pos = np.stack([t // GRID_W, t % GRID_W], axis=-1).astype(np.float32)
    freqs = jnp.asarray(ROPE_BASE, F32) ** (-jnp.arange(ROPE_PAIRS, dtype=F32) / ROPE_PAIRS)
    ang = jnp.asarray(pos)[:, :, None] * freqs
    cos, sin = jnp.cos(ang), jnp.sin(ang)
    cos_t = jnp.stack([cos, cos], axis=2).reshape(seq, MLA_ROPE)
    sin_t = jnp.stack([-sin, sin], axis=2).reshape(seq, MLA_ROPE)
    pad = jnp.zeros((seq, LANE - MLA_ROPE), F32)
    return jnp.concatenate([cos_t, pad], axis=1), jnp.concatenate([sin_t, pad], axis=1)


_ROPE_SWAP = np.arange(MLA_ROPE).reshape(2, 2, ROPE_PAIRS)[:, ::-1, :].reshape(-1)


def _mla_weights(w_in, w_uq, w_ukv):
    d = w_in.shape[0]
    c0 = MLA_Q_RANK + MLA_KV_RANK
    zpad = jnp.zeros((d, LANE - MLA_ROPE), w_in.dtype)
    kpe = w_in[:, c0:]
    w_in_x = jnp.concatenate([w_in[:, :c0], kpe, zpad, kpe[:, _ROPE_SWAP], zpad], axis=1)
    r = w_uq.shape[0]
    wq = w_uq.reshape(r, MLA_HEADS, MLA_NOPE + MLA_ROPE)
    zq = jnp.zeros((r, MLA_HEADS, LANE - MLA_ROPE), w_uq.dtype)
    main = jnp.concatenate([wq, zq], axis=2).reshape(r, MLA_HEADS * MLA_QK_PAD)
    swapped = jnp.concatenate([wq[:, :, MLA_NOPE:][:, :, _ROPE_SWAP], zq], axis=2).reshape(r, MLA_HEADS * LANE)
    wq_x = jnp.concatenate([main, swapped], axis=1)
    wkv = w_ukv.reshape(w_ukv.shape[0], MLA_HEADS, MLA_NOPE + MLA_V)
    wkv_x = jnp.concatenate([wkv[:, :, :MLA_NOPE].reshape(r, -1), wkv[:, :, MLA_NOPE:].reshape(r, -1)], axis=1)
    return w_in_x.astype(BF16), wq_x.astype(BF16), wkv_x.astype(BF16)


def _ml_conv_kernel(prev_ref, x_ref, next_ref, w_ref, q_ref, k_ref, *, seg_len, scale):
    tm = x_ref.shape[0]
    hk = ML_HEADS * ML_QK
    x = x_ref[...]
    row = lax.broadcasted_iota(jnp.int32, (tm, 1), 0)
    pos = (pl.program_id(0) * tm) % seg_len + row
    x_prev = jnp.where(row == 0, prev_ref[7:8, :], pltpu.roll(x, 1, 0))
    x_next = jnp.where(row == tm - 1, next_ref[0:1, :], pltpu.roll(x, tm - 1, 0))
    x_prev = jnp.where(pos == 0, 0.0, x_prev)
    x_next = jnp.where(pos == seg_len - 1, 0.0, x_next)
    y = x_prev * w_ref[0:1, :] + x * w_ref[1:2, :] + x_next * w_ref[2:3, :]
    y = y * jax.nn.sigmoid(y)
    q_ref[...] = (y[:, :hk] * scale).astype(BF16)
    k_ref[...] = y[:, hk:].astype(BF16)


def ml_conv_silu(qk_raw, conv_w, *, seg_len, tm=256):
    rows, c = qk_raw.shape
    tm = min(tm, seg_len)
    hk = ML_HEADS * ML_QK
    nb8 = rows // 8
    t8 = tm // 8
    return pl.pallas_call(
        functools.partial(_ml_conv_kernel, seg_len=seg_len, scale=ML_QK ** -0.5),
        grid=(rows // tm,),
        in_specs=[
            pl.BlockSpec((8, c), lambda i: (jnp.maximum(i * t8 - 1, 0), 0)),
            pl.BlockSpec((tm, c), lambda i: (i, 0)),
            pl.BlockSpec((8, c), lambda i: (jnp.minimum((i + 1) * t8, nb8 - 1), 0)),
            pl.BlockSpec((ML_CONV, c), lambda i: (0, 0)),
        ],
        out_specs=[pl.BlockSpec((tm, hk), lambda i: (i, 0)), pl.BlockSpec((tm, hk), lambda i: (i, 0))],
        out_shape=[jax.ShapeDtypeStruct((rows, hk), BF16), jax.ShapeDtypeStruct((rows, hk), BF16)],
        compiler_params=_cparams("parallel"),
        name="ml_conv_silu",
    )(qk_raw, qk_raw, qk_raw, conv_w)


def _log_sigmoid(x):
    return jnp.minimum(x, 0.0) - jnp.log(1.0 + jnp.exp(-jnp.abs(x)))


def _ml_scan_kernel(q_ref, k_ref, v_ref, g_ref, gt_ref, b_ref, bt_ref, c0_ref, n0_ref, m0_ref,
                    h_ref, c_out, n_out, m_out, c_scr, n_scr, m_scr, *, reverse):
    step = pl.program_id(1)
    tc = q_ref.shape[0]

    @pl.when(step == 0)
    def _():
        c_scr[...] = c0_ref[...]
        n_scr[...] = n0_ref[...]
        m_scr[...] = m0_ref[...]

    t_idx = lax.broadcasted_iota(jnp.int32, (tc, tc), 0)
    s_idx = lax.broadcasted_iota(jnp.int32, (tc, tc), 1)
    seen = (s_idx >= t_idx) if reverse else (s_idx <= t_idx)
    seen_t = (t_idx >= s_idx) if reverse else (t_idx <= s_idx)
    gi = 2 * ML_HEADS if reverse else 0
    gf = gi + ML_HEADS

    g = g_ref[...] + b_ref[...]
    gt = gt_ref[...] + bt_ref[...]

    for hd in range(ML_HEADS):
        qh = q_ref[:, hd * ML_QK:(hd + 1) * ML_QK]
        kh = k_ref[:, hd * ML_QK:(hd + 1) * ML_QK]
        vh = v_ref[:, hd * ML_V:(hd + 1) * ML_V]
        li_c = g[:, gi + hd:gi + hd + 1]
        lf_c = _log_sigmoid(g[:, gf + hd:gf + hd + 1])
        li_r = gt[gi + hd:gi + hd + 1, :]
        lf_r = _log_sigmoid(gt[gf + hd:gf + hd + 1, :])
        b_c = jnp.sum(jnp.where(seen, lf_r, 0.0), axis=1, keepdims=True)
        b_r = jnp.sum(jnp.where(seen_t, lf_c, 0.0), axis=0, keepdims=True)
        b_end = jnp.sum(lf_r, axis=1, keepdims=True)
        m_prev = m_scr[hd:hd + 1, 0:1]
        c_prev = c_scr[hd]
        n_prev = n_scr[hd:hd + 1, :]

        log_w = jnp.where(seen, b_c - b_r + li_r, NEG_BIG)
        log_inter = b_c + m_prev
        m_t = jnp.maximum(log_inter, jnp.max(log_w, axis=1, keepdims=True))
        w_inter = jnp.exp(log_inter - m_t)
        s = _dot_nt(qh, kh) * jnp.exp(log_w - m_t)
        num = w_inter * _dot(qh, c_prev.astype(BF16)) + _dot(s.astype(BF16), vh)
        den = (w_inter * jnp.sum(qh.astype(F32) * n_prev, axis=1, keepdims=True)
               + jnp.sum(s, axis=1, keepdims=True))
        h_ref[:, hd * ML_V:(hd + 1) * ML_V] = num / jnp.maximum(jnp.abs(den), jnp.exp(-m_t))

        log_s = b_end - b_c + li_c
        m_new = jnp.maximum(b_end + m_prev, jnp.max(log_s, axis=0, keepdims=True))
        decay = jnp.exp(b_end + m_prev - m_new)
        kw = kh.astype(F32) * jnp.exp(log_s - m_new)
        c_scr[hd] = decay * c_prev + _dot_tn(kw.astype(BF16), vh)
        n_scr[hd:hd + 1, :] = decay * n_prev + jnp.sum(kw, axis=0, keepdims=True)
        m_scr[hd:hd + 1, :] = jnp.broadcast_to(m_new, (1, LANE))

    @pl.when(step == pl.num_programs(1) - 1)
    def _():
        c_out[...] = c_scr[...]
        n_out[...] = n_scr[...]
        m_out[...] = m_scr[...]


def ml_scan(q, k, v, g, gt, b_row, b_col, state, *, batch, seg_len, reverse):
    tc = min(ML_CHUNK, seg_len)
    nc = seg_len // tc
    hk = ML_HEADS * ML_QK
    hv = ML_HEADS * ML_V
    c0, n0, m0 = state

    def blk(b, s):
        return b * nc + ((nc - 1 - s) if reverse else s)

    state_specs = [
        pl.BlockSpec((None, ML_HEADS, ML_QK, ML_V), lambda b, s: (b, 0, 0, 0)),
        pl.BlockSpec((None, ML_HEADS, ML_QK), lambda b, s: (b, 0, 0)),
        pl.BlockSpec((None, ML_HEADS, LANE), lambda b, s: (b, 0, 0)),
    ]
    return pl.pallas_call(
        functools.partial(_ml_scan_kernel, reverse=reverse),
        grid=(batch, nc),
        in_specs=[
            pl.BlockSpec((tc, hk), lambda b, s: (blk(b, s), 0)),
            pl.BlockSpec((tc, hk), lambda b, s: (blk(b, s), 0)),
            pl.BlockSpec((tc, hv), lambda b, s: (blk(b, s), 0)),
            pl.BlockSpec((tc, LANE), lambda b, s: (blk(b, s), 0)),
            pl.BlockSpec((4 * ML_HEADS, tc), lambda b, s: (0, blk(b, s))),
            pl.BlockSpec((1, LANE), lambda b, s: (0, 0)),
            pl.BlockSpec((4 * ML_HEADS, 1), lambda b, s: (0, 0)),
        ] + state_specs,
        out_specs=[pl.BlockSpec((tc, hv), lambda b, s: (blk(b, s), 0))] + state_specs,
        out_shape=[
            jax.ShapeDtypeStruct((batch * seg_len, hv), F32),
            jax.ShapeDtypeStruct(c0.shape, F32),
            jax.ShapeDtypeStruct(n0.shape, F32),
            jax.ShapeDtypeStruct(m0.shape, F32),
        ],
        scratch_shapes=[
            pltpu.VMEM((ML_HEADS, ML_QK, ML_V), F32),
            pltpu.VMEM((ML_HEADS, ML_QK), F32),
            pltpu.VMEM((ML_HEADS, LANE), F32),
        ],
        compiler_params=_cparams("parallel", "arbitrary"),
        name="ml_scan_bwd" if reverse else "ml_scan_fwd",
    )(q, k, v, g, gt, b_row, b_col, c0, n0, m0)


def _ml_out_kernel(hf_ref, hb_ref, og_ref, gh_ref, w_ref, h_ref, g_ref, mod_ref, out_ref, x_scr, *, gate_idx):
    og = og_ref[...]
    sig = jax.nn.sigmoid(og)
    for hd in range(ML_HEADS):
        sl = slice(hd * ML_V, (hd + 1) * ML_V)
        hh = hf_ref[:, sl] + hb_ref[:, sl]
        x_scr[:, sl] = (sig[:, sl] * _rms(hh, gh_ref[:, sl])).astype(BF16)
    y = _dot(x_scr[...], w_ref[...])
    out_ref[...] = h_ref[...] + mod_ref[gate_idx:gate_idx + 1, :] * _rms(y, g_ref[...])


def ml_out_residual(h_f, h_b, og, g_head, w, h, g, mod, *, gate_idx, rows_per_mod, mod_base, tm=256):
    rows, hv = h_f.shape
    d = w.shape[1]
    tm = min(tm, rows)
    return pl.pallas_call(
        functools.partial(_ml_out_kernel, gate_idx=gate_idx),
        grid=(rows // tm,),
        in_specs=[
            pl.BlockSpec((tm, hv), lambda i: (i, 0)),
            pl.BlockSpec((tm, hv), lambda i: (i, 0)),
            pl.BlockSpec((tm, hv), lambda i: (i, 0)),
            pl.BlockSpec((1, hv), lambda i: (0, 0)),
            pl.BlockSpec((hv, d), lambda i: (0, 0)),
            pl.BlockSpec((tm, d), lambda i: (i, 0)),
            pl.BlockSpec((1, d), lambda i: (0, 0)),
            pl.BlockSpec((None, N_MOD, d), _mod_index_map(tm, rows_per_mod, mod_base)),
        ],
        out_specs=pl.BlockSpec((tm, d), lambda i: (i, 0)),
        out_shape=jax.ShapeDtypeStruct((rows, d), F32),
        scratch_shapes=[pltpu.VMEM((tm, hv), BF16)],
        compiler_params=_cparams("parallel"),
        name="ml_out_residual",
    )(h_f, h_b, og, g_head.reshape(1, hv), w, h, g.reshape(1, d), mod)


def _streams(batch, seq, lc):
    return dict(rows_per_mod=seq, mod_base=0), dict(rows_per_mod=batch * lc, mod_base=batch)


def na_mixer(h_lat, h_ctx, mod, g_pre, g_post, w_qkv, rpb, w_o, *, batch, seq, lc, need_ctx):
    lat, ctx = _streams(batch, seq, lc)
    wq = w_qkv.astype(BF16)
    wo = w_o.astype(BF16)
    proj = functools.partial(norm_proj, g=g_pre, w=wq, mod=mod, shift_idx=0, scale_idx=1,
                             out_dtype=BF16, tm=512, tn=1024)
    qkv_lat = proj(h_lat, **lat)
    qkv_ctx = proj(h_ctx, **ctx)
    bias = _na_bias_tables(rpb, seq // GRID_W)
    o_lat = na_attention(qkv_lat, qkv_ctx, bias, batch=batch, seq=seq, lc=lc)
    h_lat = out_proj_residual(o_lat, wo, h_lat, g_post, mod, gate_idx=2, **lat)
    if need_ctx:
        dh = w_o.shape[0] // NA_HEADS
        o_ctx = ctx_attention(qkv_ctx, qkv_ctx, qkv_ctx, batch=batch, heads=NA_HEADS, lc=lc, dq=dh, dv=dh,
                              q_off=0, k_off=NA_HEADS, v_off=2 * NA_HEADS, scale=dh ** -0.5)
        h_ctx = out_proj_residual(o_ctx, wo, h_ctx, g_post, mod, gate_idx=2, **ctx)
    return h_lat, h_ctx


def mla_mixer(h_lat, h_ctx, mod, g_pre, g_post, w_in, g_q, g_kv, w_uq, w_ukv, w_o, *, batch, seq, lc, need_ctx):
    lat, ctx = _streams(batch, seq, lc)
    w_in_x, wq_x, wkv_x = _mla_weights(w_in, w_uq, w_ukv)
    wo = w_o.astype(BF16)
    proj = functools.partial(norm_proj, g=g_pre, w=w_in_x, mod=mod, shift_idx=0, scale_idx=1,
                             out_dtype=F32, tm=512, tn=w_in_x.shape[1])
    z_lat = proj(h_lat, **lat)
    z_ctx = proj(h_ctx, **ctx)
    cos_l, sin_l = _rope_tables(seq)
    ones = jnp.concatenate([jnp.ones((lc, MLA_ROPE), F32), jnp.zeros((lc, LANE - MLA_ROPE), F32)], axis=1)
    q_lat, k_lat, v_lat = mla_up(z_lat, g_q, g_kv, wq_x, wkv_x, cos_l, sin_l)
    q_ctx, k_ctx, v_ctx = mla_up(z_ctx, g_q, g_kv, wq_x, wkv_x, ones, jnp.zeros_like(ones))
    o_lat = mla_attention(q_lat, k_lat, v_lat, k_ctx, v_ctx, batch=batch, seq=seq, lc=lc)
    h_lat = out_proj_residual(o_lat, wo, h_lat, g_post, mod, gate_idx=2, **lat)
    if need_ctx:
        o_ctx = ctx_attention(q_ctx, k_ctx, v_ctx, batch=batch, heads=MLA_HEADS, lc=lc, dq=MLA_QK_PAD, dv=MLA_V,
                              q_off=0, k_off=0, v_off=0, scale=1.0)
        h_ctx = out_proj_residual(o_ctx, wo, h_ctx, g_post, mod, gate_idx=2, **ctx)
    return h_lat, h_ctx


def mlstm_mixer(h_lat, h_ctx, mod, g_pre, g_post, w_in, b_gate, conv_w, g_head, w_o, *, batch, seq, lc, need_ctx):
    lat, ctx = _streams(batch, seq, lc)
    d = w_in.shape[0]
    hk2 = 2 * ML_HEADS * ML_QK
    hv = ML_HEADS * ML_V
    ng = 4 * ML_HEADS
    w_qk = w_in[:, :hk2].astype(BF16)
    w_v = w_in[:, hk2:hk2 + hv].astype(BF16)
    w_og = w_in[:, hk2 + hv:hk2 + 2 * hv].astype(BF16)
    w_g = jnp.concatenate([w_in[:, hk2 + 2 * hv:], jnp.zeros((d, LANE - ng), w_in.dtype)], axis=1).astype(BF16)
    wo = w_o.astype(BF16)
    b_row = jnp.concatenate([b_gate, jnp.zeros((LANE - ng,), F32)]).reshape(1, LANE)
    b_col = b_gate.reshape(ng, 1)

    def project(h, stream, seg_len):
        proj = functools.partial(norm_proj, h, g_pre, mod=mod, shift_idx=0, scale_idx=1, tm=512, **stream)
        qk_raw = proj(w_qk, out_dtype=F32, tn=1024)
        v = proj(w_v, out_dtype=BF16, tn=1024)
        og = proj(w_og, out_dtype=F32, tn=1024)
        g = proj(w_g, out_dtype=F32, tn=LANE)
        q, k = ml_conv_silu(qk_raw, conv_w, seg_len=seg_len)
        return q, k, v, og, g, g[:, :ng].T

    def bidir(q, k, v, g, gt, st_f, st_b, seg_len):
        h_f, *st_f = ml_scan(q, k, v, g, gt, b_row, b_col, st_f, batch=batch, seg_len=seg_len, reverse=False)
        h_b, *st_b = ml_scan(q, k, v, g, gt, b_row, b_col, st_b, batch=batch, seg_len=seg_len, reverse=True)
        return h_f, h_b, st_f, st_b

    zero = (jnp.zeros((batch, ML_HEADS, ML_QK, ML_V), F32), jnp.zeros((batch, ML_HEADS, ML_QK), F32),
            jnp.zeros((batch, ML_HEADS, LANE), F32))
    qc, kc, vc, ogc, gc, gtc = project(h_ctx, ctx, lc)
    ql, kl, vl, ogl, gl, gtl = project(h_lat, lat, seq)
    hc_f, hc_b, st_f, st_b = bidir(qc, kc, vc, gc, gtc, zero, zero, lc)
    hl_f, hl_b, _, _ = bidir(ql, kl, vl, gl, gtl, st_f, st_b, seq)
    h_lat = ml_out_residual(hl_f, hl_b, ogl, g_head, wo, h_lat, g_post, mod, gate_idx=2, **lat)
    if need_ctx:
        h_ctx = ml_out_residual(hc_f, hc_b, ogc, g_head, wo, h_ctx, g_post, mod, gate_idx=2, **ctx)
    return h_lat, h_ctx


def kernel(x, c, ctx, c_ctx, ada_w, ada_b, norm_g, ff_w1, ff_w2, na_w_qkv, na_rpb, na_w_o, mla_w_in, mla_g_q, mla_g_kv, mla_w_uq, mla_w_ukv, mla_w_o, ml_w_in, ml_b_gate, ml_conv, ml_g_head, ml_w_o):
    batch, seq, d = x.shape
    lc = ctx.shape[1]
    depth = ada_w.shape[0]
    lat, cst = _streams(batch, seq, lc)

    n_cond = 8 * ((batch + 1 + 7) // 8)
    c_all = jnp.concatenate([c, c_ctx[None, :], jnp.zeros((n_cond - batch - 1, d), F32)], axis=0)
    mods = ada_modulation(c_all, ada_w, ada_b).reshape(depth, n_cond, N_MOD, d)

    h_lat = x.reshape(batch * seq, d)
    h_ctx = ctx.reshape(batch * lc, d)
    for i in range(depth):
        last = i == depth - 1
        mod = mods[i]
        g_pre1, g_post1, g_pre2, g_post2 = norm_g[i]
        kind, j = i % 3, i // 3
        dims = dict(batch=batch, seq=seq, lc=lc, need_ctx=not last)
        if kind == 0:
            h_lat, h_ctx = na_mixer(h_lat, h_ctx, mod, g_pre1, g_post1, na_w_qkv[j], na_rpb[j], na_w_o[j], **dims)
        elif kind == 1:
            h_lat, h_ctx = mla_mixer(h_lat, h_ctx, mod, g_pre1, g_post1, mla_w_in[j], mla_g_q[j], mla_g_kv[j],
                                     mla_w_uq[j], mla_w_ukv[j], mla_w_o[j], **dims)
        else:
            h_lat, h_ctx = mlstm_mixer(h_lat, h_ctx, mod, g_pre1, g_post1, ml_w_in[j], ml_b_gate[j], ml_conv[j],
                                       ml_g_head[j], ml_w_o[j], **dims)
        w1 = ff_w1[i].astype(BF16)
        w2 = ff_w2[i].astype(BF16)
        h_lat = ffn_residual(h_lat, g_pre2, g_post2, mod, w1, w2, **lat)
        if not last:
            h_ctx = ffn_residual(h_ctx, g_pre2, g_post2, mod, w1, w2, **cst)
    return h_lat.reshape(batch, seq, d)
```

```python
import functools
import math

import numpy as np
import jax
import jax.numpy as jnp
from jax import lax
from jax.experimental import pallas as pl
from jax.experimental.pallas import tpu as pltpu

F32 = jnp.float32
BF16 = jnp.bfloat16

EPS = 1e-6
N_MOD = 6
GRID_W = 64

NA_HEADS = 16
NA_WIN_R = 8
NA_WIN_C = 16
NA_Q_ROWS = 4
NA_K_ROWS = 12

MLA_HEADS = 16
MLA_Q_RANK = 512
MLA_KV_RANK = 512
MLA_NOPE = 128
MLA_ROPE = 64
MLA_V = 128
ROPE_PAIRS = MLA_ROPE // 4
ROPE_BASE = 10000.0
MLA_QK_PAD = 256

ML_HEADS = 8
ML_QK = 128
ML_V = 256
ML_CONV = 3
ML_CHUNK = 256

LANE = 128
NEG_BIG = -1e30
VMEM_LIMIT_BYTES = 56 * 1024 * 1024


def _cparams(*sem):
    return pltpu.CompilerParams(dimension_semantics=sem, vmem_limit_bytes=VMEM_LIMIT_BYTES)


def _rms(x, g):
    return x * lax.rsqrt(jnp.mean(x * x, axis=-1, keepdims=True) + EPS) * g


def _dot(a, b):
    return jnp.dot(a, b, preferred_element_type=F32)


def _dot_nt(a, b):
    return lax.dot_general(a, b, (((1,), (1,)), ((), ())), preferred_element_type=F32)


def _dot_tn(a, b):
    return lax.dot_general(a, b, (((0,), (0,)), ((), ())), preferred_element_type=F32)


def _mod_index_map(tm, rows_per_mod, mod_base):
    return lambda i, *_: (mod_base + (i * tm) // rows_per_mod, 0, 0)


def _ada_kernel(c_ref, w_ref, b_ref, o_ref):
    c = c_ref[...]
    s = c * jax.nn.sigmoid(c)
    o_ref[...] = _dot(s, w_ref[...]) + b_ref[...]


def ada_modulation(c_all, ada_w, ada_b, *, tn=1024):
    depth, d, n = ada_w.shape
    r = c_all.shape[0]
    return pl.pallas_call(
        _ada_kernel,
        grid=(depth, n // tn),
        in_specs=[
            pl.BlockSpec((r, d), lambda l, j: (0, 0)),
            pl.BlockSpec((None, d, tn), lambda l, j: (l, 0, j)),
            pl.BlockSpec((None, 1, tn), lambda l, j: (l, 0, j)),
        ],
        out_specs=pl.BlockSpec((None, r, tn), lambda l, j: (l, 0, j)),
        out_shape=jax.ShapeDtypeStruct((depth, r, n), F32),
        compiler_params=_cparams("arbitrary", "arbitrary"),
        name="ada_modulation",
    )(c_all, ada_w, ada_b.reshape(depth, 1, n))


def _norm_proj_kernel(*refs, shift_idx, scale_idx):
    if shift_idx is None:
        x_ref, g_ref, w_ref, o_ref, a_scr = refs
    else:
        x_ref, g_ref, mod_ref, w_ref, o_ref, a_scr = refs

    @pl.when(pl.program_id(1) == 0)
    def _():
        y = _rms(x_ref[...], g_ref[...])
        if shift_idx is not None:
            y = y * (1.0 + mod_ref[scale_idx:scale_idx + 1, :]) + mod_ref[shift_idx:shift_idx + 1, :]
        a_scr[...] = y.astype(BF16)

    o_ref[...] = _dot(a_scr[...], w_ref[...]).astype(o_ref.dtype)


def norm_proj(x, g, w, *, out_dtype, tm, tn, mod=None, rows_per_mod=None, mod_base=0,
              shift_idx=None, scale_idx=None, x_col_block=0):
    rows = x.shape[0]
    k, n = w.shape
    tm = min(tm, rows)
    tn = min(tn, n)
    in_specs = [
        pl.BlockSpec((tm, k), lambda i, j: (i, x_col_block)),
        pl.BlockSpec((1, k), lambda i, j: (0, 0)),
    ]
    args = [x, g.reshape(1, k)]
    if shift_idx is not None:
        in_specs.append(pl.BlockSpec((None, N_MOD, k), _mod_index_map(tm, rows_per_mod, mod_base)))
        args.append(mod)
    in_specs.append(pl.BlockSpec((k, tn), lambda i, j: (0, j)))
    args.append(w)
    return pl.pallas_call(
        functools.partial(_norm_proj_kernel, shift_idx=shift_idx, scale_idx=scale_idx),
        grid=(rows // tm, n // tn),
        in_specs=in_specs,
        out_specs=pl.BlockSpec((tm, tn), lambda i, j: (i, j)),
        out_shape=jax.ShapeDtypeStruct((rows, n), out_dtype),
        scratch_shapes=[pltpu.VMEM((tm, k), BF16)],
        compiler_params=_cparams("parallel", "arbitrary"),
        name="norm_proj",
    )(*args)


def _out_proj_kernel(o_ref, w_ref, h_ref, g_ref, mod_ref, out_ref, *, gate_idx, o_transposed):
    y = _dot_tn(o_ref[...], w_ref[...]) if o_transposed else _dot(o_ref[...], w_ref[...])
    out_ref[...] = h_ref[...] + mod_ref[gate_idx:gate_idx + 1, :] * _rms(y, g_ref[...])


def out_proj_residual(o, w, h, g, mod, *, gate_idx, rows_per_mod, mod_base, o_transposed=False, tm=512):
    k, d = w.shape
    rows = h.shape[0]
    tm = min(tm, rows)
    o_spec = pl.BlockSpec((k, tm), lambda i: (0, i)) if o_transposed else pl.BlockSpec((tm, k), lambda i: (i, 0))
    return pl.pallas_call(
        functools.partial(_out_proj_kernel, gate_idx=gate_idx, o_transposed=o_transposed),
        grid=(rows // tm,),
        in_specs=[
            o_spec,
            pl.BlockSpec((k, d), lambda i: (0, 0)),
            pl.BlockSpec((tm, d), lambda i: (i, 0)),
            pl.BlockSpec((1, d), lambda i: (0, 0)),
            pl.BlockSpec((None, N_MOD, d), _mod_index_map(tm, rows_per_mod, mod_base)),
        ],
        out_specs=pl.BlockSpec((tm, d), lambda i: (i, 0)),
        out_shape=jax.ShapeDtypeStruct((rows, d), F32),
        compiler_params=_cparams("parallel"),
        name="out_proj_residual",
    )(o, w, h, g.reshape(1, d), mod)


def _ffn_kernel(x_ref, g1_ref, mod_ref, w1_ref, w2_ref, g2_ref, out_ref, a_scr,
                *, shift_idx, scale_idx, gate_idx):
    f = pl.program_id(1)

    @pl.when(f == 0)
    def _():
        y = _rms(x_ref[...], g1_ref[...])
        y = y * (1.0 + mod_ref[scale_idx:scale_idx + 1, :]) + mod_ref[shift_idx:shift_idx + 1, :]
        a_scr[...] = y.astype(BF16)
        out_ref[...] = jnp.zeros_like(out_ref)

    hid = jnp.maximum(_dot(a_scr[...], w1_ref[...]), 0.0)
    out_ref[...] += _dot((hid * hid).astype(BF16), w2_ref[...])

    @pl.when(f == pl.num_programs(1) - 1)
    def _():
        out_ref[...] = x_ref[...] + mod_ref[gate_idx:gate_idx + 1, :] * _rms(out_ref[...], g2_ref[...])


def ffn_residual(x, g_pre, g_post, mod, w1, w2, *, rows_per_mod, mod_base, tm=1024, tf=512):
    rows, d = x.shape
    dff = w1.shape[1]
    tm = min(tm, rows)
    return pl.pallas_call(
        functools.partial(_ffn_kernel, shift_idx=3, scale_idx=4, gate_idx=5),
        grid=(rows // tm, dff // tf),
        in_specs=[
            pl.BlockSpec((tm, d), lambda i, f: (i, 0)),
            pl.BlockSpec((1, d), lambda i, f: (0, 0)),
            pl.BlockSpec((None, N_MOD, d), _mod_index_map(tm, rows_per_mod, mod_base)),
            pl.BlockSpec((d, tf), lambda i, f: (0, f)),
            pl.BlockSpec((tf, d), lambda i, f: (f, 0)),
            pl.BlockSpec((1, d), lambda i, f: (0, 0)),
        ],
        out_specs=pl.BlockSpec((tm, d), lambda i, f: (i, 0)),
        out_shape=jax.ShapeDtypeStruct((rows, d), F32),
        scratch_shapes=[pltpu.VMEM((tm, d), BF16)],
        compiler_params=_cparams("parallel", "arbitrary"),
        name="ffn_residual",
    )(x, g_pre.reshape(1, d), mod, w1, w2, g_post.reshape(1, d))


def _ctx_attn_kernel(q_ref, k_ref, v_ref, o_ref, *, scale, base2):
    s = _dot_nt(q_ref[...], k_ref[...])
    if scale != 1.0:
        s = s * scale
    m = jnp.max(s, axis=-1, keepdims=True)
    p = jnp.exp2(s - m) if base2 else jnp.exp(s - m)
    l = jnp.sum(p, axis=-1, keepdims=True)
    o_ref[...] = (_dot(p.astype(BF16), v_ref[...]) / l).astype(o_ref.dtype)


def ctx_attention(q_arr, k_arr, v_arr, *, batch, heads, lc, dq, dv, q_off, k_off, v_off, scale, base2=False):
    return pl.pallas_call(
        functools.partial(_ctx_attn_kernel, scale=scale, base2=base2),
        grid=(batch, heads),
        in_specs=[
            pl.BlockSpec((lc, dq), lambda b, h: (b, q_off + h)),
            pl.BlockSpec((lc, dq), lambda b, h: (b, k_off + h)),
            pl.BlockSpec((lc, dv), lambda b, h: (b, v_off + h)),
        ],
        out_specs=pl.BlockSpec((lc, dv), lambda b, h: (b, h)),
        out_shape=jax.ShapeDtypeStruct((batch * lc, heads * dv), BF16),
        compiler_params=_cparams("parallel", "parallel"),
        name="ctx_attention",
    )(q_arr, k_arr, v_arr)


def _na_window_tables(img_rows):
    kr = min(NA_WIN_R, img_rows)
    n_blk = img_rows // NA_Q_ROWS

    def tables(i):
        ks = int(np.clip(i * NA_Q_ROWS - NA_WIN_R // 2, 0, img_rows - NA_K_ROWS))
        r = i * NA_Q_ROWS + np.arange(NA_Q_ROWS)[:, None]
        kr_abs = ks + np.arange(NA_K_ROWS)[None, :]
        rs = np.clip(r - kr // 2, 0, img_rows - kr)
        valid = (kr_abs >= rs) & (kr_abs < rs + kr)
        dr = np.clip(kr_abs - r + NA_WIN_R - 1, 0, 2 * NA_WIN_R - 2)
        return ks, valid, dr

    first, inner, last = tables(0), tables(1), tables(n_blk - 1)
    for i in range(1, n_blk - 1):
        ks, valid, dr = tables(i)
        assert ks == i * NA_Q_ROWS - NA_WIN_R // 2
        assert (valid == inner[1]).all() and (dr[valid] == inner[2][valid]).all()
    assert first[0] == 0 and last[0] == img_rows - NA_K_ROWS
    return [first, inner, last]


def _na_bias_tables(rpb, img_rows):
    qcol = np.arange(GRID_W)
    col_start = np.clip(qcol - NA_WIN_C // 2, 0, GRID_W - NA_WIN_C)
    col_mask = (qcol[None, :] >= col_start[:, None]) & (qcol[None, :] < col_start[:, None] + NA_WIN_C)
    dc_idx = np.clip(qcol[None, :] - qcol[:, None], -(NA_WIN_C - 1), NA_WIN_C - 1) + NA_WIN_C - 1
    rpb_cols = rpb[:, :, dc_idx]
    out = []
    for _, valid, dr in _na_window_tables(img_rows):
        t = rpb_cols[:, dr]
        mask = valid[:, :, None, None] & col_mask[None, None]
        t = jnp.where(mask[None], t, NEG_BIG)
        t = t.transpose(0, 1, 3, 2, 4).reshape(rpb.shape[0], NA_Q_ROWS * GRID_W, NA_K_ROWS * GRID_W)
        out.append(t)
    return jnp.stack(out, axis=1)


def _na_kernel(q_ref, k_ref, v_ref, kc_ref, vc_ref, bias_ref, o_ref, *, img_rows, scale):
    qb = NA_Q_ROWS * GRID_W
    kb = NA_K_ROWS * GRID_W
    n_blk = img_rows // NA_Q_ROWS
    kc = kc_ref[...]
    vc = vc_ref[...]

    def block(i, variant):
        if isinstance(i, int):
            q0 = i * qb
            ks = min(max(i * NA_Q_ROWS - NA_WIN_R // 2, 0), img_rows - NA_K_ROWS) * GRID_W
        else:
            q0 = pl.multiple_of(i * qb, qb)
            ks = pl.multiple_of((i * NA_Q_ROWS - NA_WIN_R // 2) * GRID_W, GRID_W)
        q = q_ref[pl.ds(q0, qb), :]
        s_w = _dot_nt(q, k_ref[pl.ds(ks, kb), :]) * scale + bias_ref[variant]
        s_c = _dot_nt(q, kc) * scale
        m = jnp.maximum(jnp.max(s_w, axis=-1, keepdims=True), jnp.max(s_c, axis=-1, keepdims=True))
        p_w = jnp.exp(s_w - m)
        p_c = jnp.exp(s_c - m)
        l = jnp.sum(p_w, axis=-1, keepdims=True) + jnp.sum(p_c, axis=-1, keepdims=True)
        o = _dot(p_w.astype(BF16), v_ref[pl.ds(ks, kb), :]) + _dot(p_c.astype(BF16), vc)
        o_ref[pl.ds(q0, qb), :] = (o / l).astype(o_ref.dtype)

    block(0, 0)

    def body(i, carry):
        block(i, 1)
        return carry

    lax.fori_loop(1, n_blk - 1, body, 0, unroll=2)
    block(n_blk - 1, 2)


def na_attention(qkv_lat, qkv_ctx, bias, *, batch, seq, lc):
    heads = NA_HEADS
    dh = qkv_lat.shape[1] // (3 * heads)
    img_rows = seq // GRID_W
    qb = NA_Q_ROWS * GRID_W
    kb = NA_K_ROWS * GRID_W
    return pl.pallas_call(
        functools.partial(_na_kernel, img_rows=img_rows, scale=dh ** -0.5),
        grid=(batch, heads),
        in_specs=[
            pl.BlockSpec((seq, dh), lambda b, h: (b, h)),
            pl.BlockSpec((seq, dh), lambda b, h: (b, heads + h)),
            pl.BlockSpec((seq, dh), lambda b, h: (b, 2 * heads + h)),
            pl.BlockSpec((lc, dh), lambda b, h: (b, heads + h)),
            pl.BlockSpec((lc, dh), lambda b, h: (b, 2 * heads + h)),
            pl.BlockSpec((None, 3, qb, kb), lambda b, h: (h, 0, 0, 0)),
        ],
        out_specs=pl.BlockSpec((seq, dh), lambda b, h: (b, h)),
        out_shape=jax.ShapeDtypeStruct((batch * seq, heads * dh), BF16),
        compiler_params=_cparams("parallel", "parallel"),
        name="na_attention",
    )(qkv_lat, qkv_lat, qkv_lat, qkv_ctx, qkv_ctx, bias)


def _mla_up_kernel(z_ref, gq_ref, gkv_ref, wq_ref, wkv_ref, cos_ref, sin_ref,
                   q_ref, k_ref, vt_ref, *maybe_v_ref, scale):
    hq = MLA_HEADS * MLA_QK_PAD
    z = z_ref[...]
    cq = _rms(z[:, :MLA_Q_RANK], gq_ref[...]).astype(BF16)
    ckv = _rms(z[:, MLA_Q_RANK:MLA_Q_RANK + MLA_KV_RANK], gkv_ref[...]).astype(BF16)
    cos = cos_ref[...]
    sin = sin_ref[...]
    c0 = MLA_Q_RANK + MLA_KV_RANK
    kpe = (z[:, c0:c0 + LANE] * cos + z[:, c0 + LANE:c0 + 2 * LANE] * sin).astype(BF16)
    qraw = _dot(cq, wq_ref[...])
    kv = _dot(ckv, wkv_ref[...])
    for h in range(MLA_HEADS):
        a = h * MLA_QK_PAD
        q_ref[:, a:a + LANE] = (qraw[:, a:a + LANE] * scale).astype(BF16)
        pe = qraw[:, a + LANE:a + 2 * LANE]
        sw = qraw[:, hq + h * LANE:hq + (h + 1) * LANE]
        q_ref[:, a + LANE:a + 2 * LANE] = ((pe * cos + sw * sin) * scale).astype(BF16)
        k_ref[:, a:a + LANE] = kv[:, h * MLA_NOPE:(h + 1) * MLA_NOPE].astype(BF16)
        k_ref[:, a + LANE:a + 2 * LANE] = kpe
    v = kv[:, MLA_HEADS * MLA_NOPE:]
    vt_ref[...] = v.T.astype(BF16)
    for v_ref in maybe_v_ref:
        v_ref[...] = v.astype(BF16)


def mla_up(z, g_q, g_kv, wq, wkv, cos, sin, *, with_v, tm=256):
    rows = z.shape[0]
    tm = min(tm, rows)
    n_pos = cos.shape[0] // tm
    hq = MLA_HEADS * MLA_QK_PAD
    hv = MLA_HEADS * MLA_V
    v_specs = [pl.BlockSpec((tm, hv), lambda i: (i, 0))] if with_v else []
    v_shapes = [jax.ShapeDtypeStruct((rows, hv), BF16)] if with_v else []
    return pl.pallas_call(
        functools.partial(_mla_up_kernel, scale=(MLA_NOPE + MLA_ROPE) ** -0.5 * math.log2(math.e)),
        grid=(rows // tm,),
        in_specs=[
            pl.BlockSpec((tm, z.shape[1]), lambda i: (i, 0)),
            pl.BlockSpec((1, MLA_Q_RANK), lambda i: (0, 0)),
            pl.BlockSpec((1, MLA_KV_RANK), lambda i: (0, 0)),
            pl.BlockSpec(wq.shape, lambda i: (0, 0)),
            pl.BlockSpec(wkv.shape, lambda i: (0, 0)),
            pl.BlockSpec((tm, LANE), lambda i: (i % n_pos, 0)),
            pl.BlockSpec((tm, LANE), lambda i: (i % n_pos, 0)),
        ],
        out_specs=[
            pl.BlockSpec((tm, hq), lambda i: (i, 0)),
            pl.BlockSpec((tm, hq), lambda i: (i, 0)),
            pl.BlockSpec((hv, tm), lambda i: (0, i)),
        ] + v_specs,
        out_shape=[
            jax.ShapeDtypeStruct((rows, hq), BF16),
            jax.ShapeDtypeStruct((rows, hq), BF16),
            jax.ShapeDtypeStruct((hv, rows), BF16),
        ] + v_shapes,
        compiler_params=_cparams("parallel"),
        name="mla_up",
    )(z, g_q.reshape(1, -1), g_kv.reshape(1, -1), wq, wkv, cos, sin)


def _mla_attn_kernel(q_ref, k_ref, vt_ref, kc_ref, vct_ref, ot_ref, *, tk):
    q = q_ref[...]

    def logits(c):
        return _dot_nt(k_ref[c * tk:(c + 1) * tk, :], q)

    def tile(st, vt, state):
        m_tile = jnp.max(st, axis=0, keepdims=True)
        if state is None:
            m_new = m_tile
        else:
            m, l, acc = state
            m_new = jnp.maximum(m, m_tile)
            alpha = jnp.exp2(m - m_new)
        p = jnp.exp2(st - m_new)
        p_sum = jnp.sum(p, axis=0, keepdims=True)
        pv = _dot(vt, p.astype(BF16))
        if state is None:
            return m_new, p_sum, pv
        return m_new, alpha * l + p_sum, alpha * acc + pv

    n_kv = k_ref.shape[0] // tk
    st_next = logits(0)
    state = tile(_dot_nt(kc_ref[...], q), vct_ref[...], None)
    for c in range(n_kv):
        st = st_next
        if c + 1 < n_kv:
            st_next = logits(c + 1)
        state = tile(st, vt_ref[:, c * tk:(c + 1) * tk], state)
    _, l, acc = state
    ot_ref[...] = (acc / l).astype(ot_ref.dtype)


def mla_attention(q_lat, k_lat, vt_lat, k_ctx, vt_ctx, *, batch, seq, lc, tq=512, tk=1024):
    heads = MLA_HEADS
    tq = min(tq, seq)
    tk = min(tk, seq)
    nq = seq // tq
    return pl.pallas_call(
        functools.partial(_mla_attn_kernel, tk=tk),
        grid=(batch, heads, nq),
        in_specs=[
            pl.BlockSpec((tq, MLA_QK_PAD), lambda b, h, i: (b * nq + i, h)),
            pl.BlockSpec((seq, MLA_QK_PAD), lambda b, h, i: (b, h)),
            pl.BlockSpec((MLA_V, seq), lambda b, h, i: (h, b)),
            pl.BlockSpec((lc, MLA_QK_PAD), lambda b, h, i: (b, h)),
            pl.BlockSpec((MLA_V, lc), lambda b, h, i: (h, b)),
        ],
        out_specs=pl.BlockSpec((MLA_V, tq), lambda b, h, i: (h, b * nq + i)),
        out_shape=jax.ShapeDtypeStruct((heads * MLA_V, batch * seq), BF16),
        compiler_params=_cparams("parallel", "parallel", "arbitrary"),
        name="mla_attention",
    )(q_lat, k_lat, vt_lat, k_ctx, vt_ctx)


def _rope_tables(seq):
    t = np.arange(seq)
    pos = np.stack([t // GRID_W, t % GRID_W], axis=-1).astype(np.float32)
    freqs = jnp.asarray(ROPE_BASE, F32) ** (-jnp.arange(ROPE_PAIRS, dtype=F32) / ROPE_PAIRS)
    ang = jnp.asarray(pos)[:, :, None] * freqs
    cos, sin = jnp.cos(ang), jnp.sin(ang)
    cos_t = jnp.stack([cos, cos], axis=2).reshape(seq, MLA_ROPE)
    sin_t = jnp.stack([-sin, sin], axis=2).reshape(seq, MLA_ROPE)
    pad = jnp.zeros((seq, LANE - MLA_ROPE), F32)
    return jnp.concatenate([cos_t, pad], axis=1), jnp.concatenate([sin_t, pad], axis=1)


_ROPE_SWAP = np.arange(MLA_ROPE).reshape(2, 2, ROPE_PAIRS)[:, ::-1, :].reshape(-1)


def _mla_weights(w_in, w_uq, w_ukv):
    d = w_in.shape[0]
    c0 = MLA_Q_RANK + MLA_KV_RANK
    zpad = jnp.zeros((d, LANE - MLA_ROPE), w_in.dtype)
    kpe = w_in[:, c0:]
    w_in_x = jnp.concatenate([w_in[:, :c0], kpe, zpad, kpe[:, _ROPE_SWAP], zpad], axis=1)
    r = w_uq.shape[0]
    wq = w_uq.reshape(r, MLA_HEADS, MLA_NOPE + MLA_ROPE)
    zq = jnp.zeros((r, MLA_HEADS, LANE - MLA_ROPE), w_uq.dtype)
    main = jnp.concatenate([wq, zq], axis=2).reshape(r, MLA_HEADS * MLA_QK_PAD)
    swapped = jnp.concatenate([wq[:, :, MLA_NOPE:][:, :, _ROPE_SWAP], zq], axis=2).reshape(r, MLA_HEADS * LANE)
    wq_x = jnp.concatenate([main, swapped], axis=1)
    wkv = w_ukv.reshape(w_ukv.shape[0], MLA_HEADS, MLA_NOPE + MLA_V)
    wkv_x = jnp.concatenate([wkv[:, :, :MLA_NOPE].reshape(r, -1), wkv[:, :, MLA_NOPE:].reshape(r, -1)], axis=1)
    return w_in_x.astype(BF16), wq_x.astype(BF16), wkv_x.astype(BF16)


def _ml_conv_kernel(prev_ref, x_ref, next_ref, w_ref, q_ref, k_ref, *, seg_len, scale):
    tm = x_ref.shape[0]
    hk = ML_HEADS * ML_QK
    x = x_ref[...]
    row = lax.broadcasted_iota(jnp.int32, (tm, 1), 0)
    pos = (pl.program_id(0) * tm) % seg_len + row
    x_prev = jnp.where(row == 0, prev_ref[7:8, :], pltpu.roll(x, 1, 0))
    x_next = jnp.where(row == tm - 1, next_ref[0:1, :], pltpu.roll(x, tm - 1, 0))
    x_prev = jnp.where(pos == 0, 0.0, x_prev)
    x_next = jnp.where(pos == seg_len - 1, 0.0, x_next)
    y = x_prev * w_ref[0:1, :] + x * w_ref[1:2, :] + x_next * w_ref[2:3, :]
    y = y * jax.nn.sigmoid(y)
    q_ref[...] = (y[:, :hk] * scale).astype(BF16)
    k_ref[...] = y[:, hk:].astype(BF16)


def ml_conv_silu(qk_raw, conv_w, *, seg_len, tm=256):
    rows, c = qk_raw.shape
    tm = min(tm, seg_len)
    hk = ML_HEADS * ML_QK
    nb8 = rows // 8
    t8 = tm // 8
    return pl.pallas_call(
        functools.partial(_ml_conv_kernel, seg_len=seg_len, scale=ML_QK ** -0.5),
        grid=(rows // tm,),
        in_specs=[
            pl.BlockSpec((8, c), lambda i: (jnp.maximum(i * t8 - 1, 0), 0)),
            pl.BlockSpec((tm, c), lambda i: (i, 0)),
            pl.BlockSpec((8, c), lambda i: (jnp.minimum((i + 1) * t8, nb8 - 1), 0)),
            pl.BlockSpec((ML_CONV, c), lambda i: (0, 0)),
        ],
        out_specs=[pl.BlockSpec((tm, hk), lambda i: (i, 0)), pl.BlockSpec((tm, hk), lambda i: (i, 0))],
        out_shape=[jax.ShapeDtypeStruct((rows, hk), BF16), jax.ShapeDtypeStruct((rows, hk), BF16)],
        compiler_params=_cparams("parallel"),
        name="ml_conv_silu",
    )(qk_raw, qk_raw, qk_raw, conv_w)


def _log_sigmoid(x):
    return jnp.minimum(x, 0.0) - jnp.log(1.0 + jnp.exp(-jnp.abs(x)))


def _ml_scan_kernel(q_ref, k_ref, v_ref, g_ref, gt_ref, b_ref, bt_ref, c0_ref, n0_ref, m0_ref,
                    h_ref, c_out, n_out, m_out, c_scr, n_scr, m_scr, *, reverse):
    step = pl.program_id(1)
    tc = q_ref.shape[0]

    @pl.when(step == 0)
    def _():
        c_scr[...] = c0_ref[...]
        n_scr[...] = n0_ref[...]
        m_scr[...] = m0_ref[...]

    t_idx = lax.broadcasted_iota(jnp.int32, (tc, tc), 0)
    s_idx = lax.broadcasted_iota(jnp.int32, (tc, tc), 1)
    seen = (s_idx >= t_idx) if reverse else (s_idx <= t_idx)
    seen_t = (t_idx >= s_idx) if reverse else (t_idx <= s_idx)
    gi = 2 * ML_HEADS if reverse else 0
    gf = gi + ML_HEADS

    g = g_ref[...] + b_ref[...]
    gt = gt_ref[...] + bt_ref[...]

    for hd in range(ML_HEADS):
        qh = q_ref[:, hd * ML_QK:(hd + 1) * ML_QK]
        kh = k_ref[:, hd * ML_QK:(hd + 1) * ML_QK]
        vh = v_ref[:, hd * ML_V:(hd + 1) * ML_V]
        li_c = g[:, gi + hd:gi + hd + 1]
        lf_c = _log_sigmoid(g[:, gf + hd:gf + hd + 1])
        li_r = gt[gi + hd:gi + hd + 1, :]
        lf_r = _log_sigmoid(gt[gf + hd:gf + hd + 1, :])
        b_c = jnp.sum(jnp.where(seen, lf_r, 0.0), axis=1, keepdims=True)
        b_r = jnp.sum(jnp.where(seen_t, lf_c, 0.0), axis=0, keepdims=True)
        b_end = jnp.sum(lf_r, axis=1, keepdims=True)
        m_prev = m_scr[hd:hd + 1, 0:1]
        c_prev = c_scr[hd]
        n_prev = n_scr[hd:hd + 1, :]

        log_w = jnp.where(seen, b_c - b_r + li_r, NEG_BIG)
        log_inter = b_c + m_prev
        m_t = jnp.maximum(log_inter, jnp.max(log_w, axis=1, keepdims=True))
        w_inter = jnp.exp(log_inter - m_t)
        s = _dot_nt(qh, kh) * jnp.exp(log_w - m_t)
        num = w_inter * _dot(qh, c_prev.astype(BF16)) + _dot(s.astype(BF16), vh)
        den = (w_inter * jnp.sum(qh.astype(F32) * n_prev, axis=1, keepdims=True)
               + jnp.sum(s, axis=1, keepdims=True))
        h_ref[:, hd * ML_V:(hd + 1) * ML_V] = num / jnp.maximum(jnp.abs(den), jnp.exp(-m_t))

        log_s = b_end - b_c + li_c
        m_new = jnp.maximum(b_end + m_prev, jnp.max(log_s, axis=0, keepdims=True))
        decay = jnp.exp(b_end + m_prev - m_new)
        kw = kh.astype(F32) * jnp.exp(log_s - m_new)
        c_scr[hd] = decay * c_prev + _dot_tn(kw.astype(BF16), vh)
        n_scr[hd:hd + 1, :] = decay * n_prev + jnp.sum(kw, axis=0, keepdims=True)
        m_scr[hd:hd + 1, :] = jnp.broadcast_to(m_new, (1, LANE))

    @pl.when(step == pl.num_programs(1) - 1)
    def _():
        c_out[...] = c_scr[...]
        n_out[...] = n_scr[...]
        m_out[...] = m_scr[...]


def ml_scan(q, k, v, g, gt, b_row, b_col, state, *, batch, seg_len, reverse):
    tc = min(ML_CHUNK, seg_len)
    nc = seg_len // tc
    hk = ML_HEADS * ML_QK
    hv = ML_HEADS * ML_V
    c0, n0, m0 = state

    def blk(b, s):
        return b * nc + ((nc - 1 - s) if reverse else s)

    state_specs = [
        pl.BlockSpec((None, ML_HEADS, ML_QK, ML_V), lambda b, s: (b, 0, 0, 0)),
        pl.BlockSpec((None, ML_HEADS, ML_QK), lambda b, s: (b, 0, 0)),
        pl.BlockSpec((None, ML_HEADS, LANE), lambda b, s: (b, 0, 0)),
    ]
    return pl.pallas_call(
        functools.partial(_ml_scan_kernel, reverse=reverse),
        grid=(batch, nc),
        in_specs=[
            pl.BlockSpec((tc, hk), lambda b, s: (blk(b, s), 0)),
            pl.BlockSpec((tc, hk), lambda b, s: (blk(b, s), 0)),
            pl.BlockSpec((tc, hv), lambda b, s: (blk(b, s), 0)),
            pl.BlockSpec((tc, LANE), lambda b, s: (blk(b, s), 0)),
            pl.BlockSpec((4 * ML_HEADS, tc), lambda b, s: (0, blk(b, s))),
            pl.BlockSpec((1, LANE), lambda b, s: (0, 0)),
            pl.BlockSpec((4 * ML_HEADS, 1), lambda b, s: (0, 0)),
        ] + state_specs,
        out_specs=[pl.BlockSpec((tc, hv), lambda b, s: (blk(b, s), 0))] + state_specs,
        out_shape=[
            jax.ShapeDtypeStruct((batch * seg_len, hv), F32),
            jax.ShapeDtypeStruct(c0.shape, F32),
            jax.ShapeDtypeStruct(n0.shape, F32),
            jax.ShapeDtypeStruct(m0.shape, F32),
        ],
        scratch_shapes=[
            pltpu.VMEM((ML_HEADS, ML_QK, ML_V), F32),
            pltpu.VMEM((ML_HEADS, ML_QK), F32),
            pltpu.VMEM((ML_HEADS, LANE), F32),
        ],
        compiler_params=_cparams("parallel", "arbitrary"),
        name="ml_scan_bwd" if reverse else "ml_scan_fwd",
    )(q, k, v, g, gt, b_row, b_col, c0, n0, m0)


def _ml_out_kernel(hf_ref, hb_ref, og_ref, gh_ref, w_ref, h_ref, g_ref, mod_ref, out_ref, x_scr, *, gate_idx):
    og = og_ref[...]
    sig = jax.nn.sigmoid(og)
    for hd in range(ML_HEADS):
        sl = slice(hd * ML_V, (hd + 1) * ML_V)
        hh = hf_ref[:, sl] + hb_ref[:, sl]
        x_scr[:, sl] = (sig[:, sl] * _rms(hh, gh_ref[:, sl])).astype(BF16)
    y = _dot(x_scr[...], w_ref[...])
    out_ref[...] = h_ref[...] + mod_ref[gate_idx:gate_idx + 1, :] * _rms(y, g_ref[...])


def ml_out_residual(h_f, h_b, og, g_head, w, h, g, mod, *, gate_idx, rows_per_mod, mod_base, tm=256):
    rows, hv = h_f.shape
    d = w.shape[1]
    tm = min(tm, rows)
    return pl.pallas_call(
        functools.partial(_ml_out_kernel, gate_idx=gate_idx),
        grid=(rows // tm,),
        in_specs=[
            pl.BlockSpec((tm, hv), lambda i: (i, 0)),
            pl.BlockSpec((tm, hv), lambda i: (i, 0)),
            pl.BlockSpec((tm, hv), lambda i: (i, 0)),
            pl.BlockSpec((1, hv), lambda i: (0, 0)),
            pl.BlockSpec((hv, d), lambda i: (0, 0)),
            pl.BlockSpec((tm, d), lambda i: (i, 0)),
            pl.BlockSpec((1, d), lambda i: (0, 0)),
            pl.BlockSpec((None, N_MOD, d), _mod_index_map(tm, rows_per_mod, mod_base)),
        ],
        out_specs=pl.BlockSpec((tm, d), lambda i: (i, 0)),
        out_shape=jax.ShapeDtypeStruct((rows, d), F32),
        scratch_shapes=[pltpu.VMEM((tm, hv), BF16)],
        compiler_params=_cparams("parallel"),
        name="ml_out_residual",
    )(h_f, h_b, og, g_head.reshape(1, hv), w, h, g.reshape(1, d), mod)


def _streams(batch, seq, lc):
    return dict(rows_per_mod=seq, mod_base=0), dict(rows_per_mod=batch * lc, mod_base=batch)


def na_mixer(h_lat, h_ctx, mod, g_pre, g_post, w_qkv, rpb, w_o, *, batch, seq, lc, need_ctx):
    lat, ctx = _streams(batch, seq, lc)
    wq = w_qkv.astype(BF16)
    wo = w_o.astype(BF16)
    proj = functools.partial(norm_proj, g=g_pre, w=wq, mod=mod, shift_idx=0, scale_idx=1,
                             out_dtype=BF16, tm=512, tn=1024)
    qkv_lat = proj(h_lat, **lat)
    qkv_ctx = proj(h_ctx, **ctx)
    bias = _na_bias_tables(rpb, seq // GRID_W)
    o_lat = na_attention(qkv_lat, qkv_ctx, bias, batch=batch, seq=seq, lc=lc)
    h_lat = out_proj_residual(o_lat, wo, h_lat, g_post, mod, gate_idx=2, **lat)
    if need_ctx:
        dh = w_o.shape[0] // NA_HEADS
        o_ctx = ctx_attention(qkv_ctx, qkv_ctx, qkv_ctx, batch=batch, heads=NA_HEADS, lc=lc, dq=dh, dv=dh,
                              q_off=0, k_off=NA_HEADS, v_off=2 * NA_HEADS, scale=dh ** -0.5)
        h_ctx = out_proj_residual(o_ctx, wo, h_ctx, g_post, mod, gate_idx=2, **ctx)
    return h_lat, h_ctx


def mla_mixer(h_lat, h_ctx, mod, g_pre, g_post, w_in, g_q, g_kv, w_uq, w_ukv, w_o, *, batch, seq, lc, need_ctx):
    lat, ctx = _streams(batch, seq, lc)
    w_in_x, wq_x, wkv_x = _mla_weights(w_in, w_uq, w_ukv)
    wo = w_o.astype(BF16)
    proj = functools.partial(norm_proj, g=g_pre, w=w_in_x, mod=mod, shift_idx=0, scale_idx=1,
                             out_dtype=F32, tm=512, tn=w_in_x.shape[1])
    z_lat = proj(h_lat, **lat)
    z_ctx = proj(h_ctx, **ctx)
    cos_l, sin_l = _rope_tables(seq)
    ones = jnp.concatenate([jnp.ones((lc, MLA_ROPE), F32), jnp.zeros((lc, LANE - MLA_ROPE), F32)], axis=1)
    q_lat, k_lat, vt_lat = mla_up(z_lat, g_q, g_kv, wq_x, wkv_x, cos_l, sin_l, with_v=False)
    q_ctx, k_ctx, vt_ctx, v_ctx = mla_up(z_ctx, g_q, g_kv, wq_x, wkv_x, ones, jnp.zeros_like(ones), with_v=True)
    ot_lat = mla_attention(q_lat, k_lat, vt_lat, k_ctx, vt_ctx, batch=batch, seq=seq, lc=lc)
    h_lat = out_proj_residual(ot_lat, wo, h_lat, g_post, mod, gate_idx=2, o_transposed=True, **lat)
    if need_ctx:
        o_ctx = ctx_attention(q_ctx, k_ctx, v_ctx, batch=batch, heads=MLA_HEADS, lc=lc, dq=MLA_QK_PAD, dv=MLA_V,
                              q_off=0, k_off=0, v_off=0, scale=1.0, base2=True)
        h_ctx = out_proj_residual(o_ctx, wo, h_ctx, g_post, mod, gate_idx=2, **ctx)
    return h_lat, h_ctx


def mlstm_mixer(h_lat, h_ctx, mod, g_pre, g_post, w_in, b_gate, conv_w, g_head, w_o, *, batch, seq, lc, need_ctx):
    lat, ctx = _streams(batch, seq, lc)
    d = w_in.shape[0]
    hk2 = 2 * ML_HEADS * ML_QK
    hv = ML_HEADS * ML_V
    ng = 4 * ML_HEADS
    w_qk = w_in[:, :hk2].astype(BF16)
    w_v = w_in[:, hk2:hk2 + hv].astype(BF16)
    w_og = w_in[:, hk2 + hv:hk2 + 2 * hv].astype(BF16)
    w_g = jnp.concatenate([w_in[:, hk2 + 2 * hv:], jnp.zeros((d, LANE - ng), w_in.dtype)], axis=1).astype(BF16)
    wo = w_o.astype(BF16)
    b_row = jnp.concatenate([b_gate, jnp.zeros((LANE - ng,), F32)]).reshape(1, LANE)
    b_col = b_gate.reshape(ng, 1)

    def project(h, stream, seg_len):
        proj = functools.partial(norm_proj, h, g_pre, mod=mod, shift_idx=0, scale_idx=1, tm=512, **stream)
        qk_raw = proj(w_qk, out_dtype=F32, tn=1024)
        v = proj(w_v, out_dtype=BF16, tn=1024)
        og = proj(w_og, out_dtype=F32, tn=1024)
        g = proj(w_g, out_dtype=F32, tn=LANE)
        q, k = ml_conv_silu(qk_raw, conv_w, seg_len=seg_len)
        return q, k, v, og, g, g[:, :ng].T

    def bidir(q, k, v, g, gt, st_f, st_b, seg_len):
        h_f, *st_f = ml_scan(q, k, v, g, gt, b_row, b_col, st_f, batch=batch, seg_len=seg_len, reverse=False)
        h_b, *st_b = ml_scan(q, k, v, g, gt, b_row, b_col, st_b, batch=batch, seg_len=seg_len, reverse=True)
        return h_f, h_b, st_f, st_b

    zero = (jnp.zeros((batch, ML_HEADS, ML_QK, ML_V), F32), jnp.zeros((batch, ML_HEADS, ML_QK), F32),
            jnp.zeros((batch, ML_HEADS, LANE), F32))
    qc, kc, vc, ogc, gc, gtc = project(h_ctx, ctx, lc)
    ql, kl, vl, ogl, gl, gtl = project(h_lat, lat, seq)
    hc_f, hc_b, st_f, st_b = bidir(qc, kc, vc, gc, gtc, zero, zero, lc)
    hl_f, hl_b, _, _ = bidir(ql, kl, vl, gl, gtl, st_f, st_b, seq)
    h_lat = ml_out_residual(hl_f, hl_b, ogl, g_head, wo, h_lat, g_post, mod, gate_idx=2, **lat)
    if need_ctx:
        h_ctx = ml_out_residual(hc_f, hc_b, ogc, g_head, wo, h_ctx, g_post, mod, gate_idx=2, **ctx)
    return h_lat, h_ctx


def kernel(x, c, ctx, c_ctx, ada_w, ada_b, norm_g, ff_w1, ff_w2, na_w_qkv, na_rpb, na_w_o, mla_w_in, mla_g_q, mla_g_kv, mla_w_uq, mla_w_ukv, mla_w_o, ml_w_in, ml_b_gate, ml_conv, ml_g_head, ml_w_o):
    batch, seq, d = x.shape
    lc = ctx.shape[1]
    depth = ada_w.shape[0]
    lat, cst = _streams(batch, seq, lc)

    n_cond = 8 * ((batch + 1 + 7) // 8)
    c_all = jnp.concatenate([c, c_ctx[None, :], jnp.zeros((n_cond - batch - 1, d), F32)], axis=0)
    mods = ada_modulation(c_all, ada_w, ada_b).reshape(depth, n_cond, N_MOD, d)

    h_lat = x.reshape(batch * seq, d)
    h_ctx = ctx.reshape(batch * lc, d)
    for i in range(depth):
        last = i == depth - 1
        mod = mods[i]
        g_pre1, g_post1, g_pre2, g_post2 = norm_g[i]
        kind, j = i % 3, i // 3
        dims = dict(batch=batch, seq=seq, lc=lc, need_ctx=not last)
        if kind == 0:
            h_lat, h_ctx = na_mixer(h_lat, h_ctx, mod, g_pre1, g_post1, na_w_qkv[j], na_rpb[j], na_w_o[j], **dims)
        elif kind == 1:
            h_lat, h_ctx = mla_mixer(h_lat, h_ctx, mod, g_pre1, g_post1, mla_w_in[j], mla_g_q[j], mla_g_kv[j],
                                     mla_w_uq[j], mla_w_ukv[j], mla_w_o[j], **dims)
        else:
            h_lat, h_ctx = mlstm_mixer(h_lat, h_ctx, mod, g_pre1, g_post1, ml_w_in[j], ml_b_gate[j], ml_conv[j],
                                       ml_g_head[j], ml_w_o[j], **dims)
        w1 = ff_w1[i].astype(BF16)
        w2 = ff_w2[i].astype(BF16)
        h_lat = ffn_residual(h_lat, g_pre2, g_post2, mod, w1, w2, **lat)
        if not last:
            h_ctx = ffn_residual(h_ctx, g_pre2, g_post2, mod, w1, w2, **cst)
    return h_lat.reshape(batch, seq, d)
```

```python
import functools
import math
from typing import Any, NamedTuple

import numpy as np
import jax
import jax.numpy as jnp
from jax import lax
from jax.experimental import pallas as pl
from jax.experimental.pallas import tpu as pltpu

F32 = jnp.float32
BF16 = jnp.bfloat16

EPS = 1e-6
N_MOD = 6
GRID_W = 64

NA_HEADS = 16
NA_WIN_R = 8
NA_WIN_C = 16
NA_Q_ROWS = 4
NA_K_ROWS = 12

MLA_HEADS = 16
MLA_Q_RANK = 512
MLA_KV_RANK = 512
MLA_NOPE = 128
MLA_ROPE = 64
MLA_V = 128
ROPE_PAIRS = MLA_ROPE // 4
ROPE_BASE = 10000.0
MLA_QK_PAD = 256

ML_HEADS = 8
ML_QK = 128
ML_V = 256
ML_CONV = 3
ML_CHUNK = 256

LANE = 128
NEG_BIG = -1e30
VMEM_LIMIT_BYTES = 56 * 1024 * 1024


def _cparams(*sem):
    return pltpu.CompilerParams(dimension_semantics=sem, vmem_limit_bytes=VMEM_LIMIT_BYTES)


def _rms(x, g):
    return x * lax.rsqrt(jnp.mean(x * x, axis=-1, keepdims=True) + EPS) * g


def _dot(a, b):
    return jnp.dot(a, b, preferred_element_type=F32)


def _dot_nt(a, b):
    return lax.dot_general(a, b, (((1,), (1,)), ((), ())), preferred_element_type=F32)


def _dot_tn(a, b):
    return lax.dot_general(a, b, (((0,), (0,)), ((), ())), preferred_element_type=F32)


def _mod_index_map(tm, rows_per_mod, mod_base):
    return lambda i, *_: (mod_base + (i * tm) // rows_per_mod, 0, 0)


def _ada_kernel(c_ref, w_ref, b_ref, o_ref):
    c = c_ref[...]
    s = c * jax.nn.sigmoid(c)
    o_ref[...] = _dot(s, w_ref[...]) + b_ref[...]


def ada_modulation(c_all, ada_w, ada_b, *, tn=1024):
    depth, d, n = ada_w.shape
    r = c_all.shape[0]
    return pl.pallas_call(
        _ada_kernel,
        grid=(depth, n // tn),
        in_specs=[
            pl.BlockSpec((r, d), lambda l, j: (0, 0)),
            pl.BlockSpec((None, d, tn), lambda l, j: (l, 0, j)),
            pl.BlockSpec((None, 1, tn), lambda l, j: (l, 0, j)),
        ],
        out_specs=pl.BlockSpec((None, r, tn), lambda l, j: (l, 0, j)),
        out_shape=jax.ShapeDtypeStruct((depth, r, n), F32),
        compiler_params=_cparams("arbitrary", "arbitrary"),
        name="ada_modulation",
    )(c_all, ada_w, ada_b.reshape(depth, 1, n))


class ProjOut(NamedTuple):
    w: jax.Array
    dtype: Any
    transposed: bool = False
    scale: float = 1.0


def _norm_proj_kernel(*refs, outs, narrow, shift_idx, scale_idx):
    n_w = len(outs) + (narrow is not None)
    n_in = 2 + (shift_idx is not None)
    x_ref, g_ref = refs[:2]
    mod_ref = refs[2] if shift_idx is not None else None
    w_refs = refs[n_in:n_in + n_w]
    o_refs = refs[n_in + n_w:n_in + 2 * n_w]
    a_scr = refs[-1]

    def emit(o_ref, w_ref, spec):
        y = _dot(a_scr[...], w_ref[...])
        if spec.scale != 1.0:
            y = y * spec.scale
        o_ref[...] = (y.T if spec.transposed else y).astype(o_ref.dtype)

    @pl.when(pl.program_id(1) == 0)
    def _():
        y = _rms(x_ref[...], g_ref[...])
        if shift_idx is not None:
            y = y * (1.0 + mod_ref[scale_idx:scale_idx + 1, :]) + mod_ref[shift_idx:shift_idx + 1, :]
        a_scr[...] = y.astype(BF16)
        if narrow is not None:
            emit(o_refs[-1], w_refs[-1], narrow)

    for o_ref, w_ref, spec in zip(o_refs, w_refs, outs):
        emit(o_ref, w_ref, spec)


def norm_proj(x, g, outs, *, tm, tn, narrow=None, mod=None, rows_per_mod=None, mod_base=0,
              shift_idx=None, scale_idx=None, x_col_block=0):
    rows = x.shape[0]
    k, n = outs[0].w.shape
    assert all(o.w.shape == (k, n) for o in outs)
    tm = min(tm, rows)
    tn = min(tn, n)
    in_specs = [
        pl.BlockSpec((tm, k), lambda i, j: (i, x_col_block)),
        pl.BlockSpec((1, k), lambda i, j: (0, 0)),
    ]
    args = [x, g.reshape(1, k)]
    if shift_idx is not None:
        in_specs.append(pl.BlockSpec((None, N_MOD, k), _mod_index_map(tm, rows_per_mod, mod_base)))
        args.append(mod)
    out_specs, out_shapes = [], []
    for o in outs:
        in_specs.append(pl.BlockSpec((k, tn), lambda i, j: (0, j)))
        args.append(o.w)
        if o.transposed:
            out_specs.append(pl.BlockSpec((tn, tm), lambda i, j: (j, i)))
            out_shapes.append(jax.ShapeDtypeStruct((n, rows), o.dtype))
        else:
            out_specs.append(pl.BlockSpec((tm, tn), lambda i, j: (i, j)))
            out_shapes.append(jax.ShapeDtypeStruct((rows, n), o.dtype))
    if narrow is not None:
        assert not narrow.transposed
        nn = narrow.w.shape[1]
        in_specs.append(pl.BlockSpec((k, nn), lambda i, j: (0, 0)))
        args.append(narrow.w)
        out_specs.append(pl.BlockSpec((tm, nn), lambda i, j: (i, 0)))
        out_shapes.append(jax.ShapeDtypeStruct((rows, nn), narrow.dtype))
    strip = lambda o: o._replace(w=None)
    return pl.pallas_call(
        functools.partial(_norm_proj_kernel, outs=tuple(strip(o) for o in outs),
                          narrow=None if narrow is None else strip(narrow),
                          shift_idx=shift_idx, scale_idx=scale_idx),
        grid=(rows // tm, n // tn),
        in_specs=in_specs,
        out_specs=out_specs,
        out_shape=out_shapes,
        scratch_shapes=[pltpu.VMEM((tm, k), BF16)],
        compiler_params=_cparams("parallel", "arbitrary"),
        name="norm_proj",
    )(*args)


def _out_proj_kernel(o_ref, w_ref, h_ref, g_ref, mod_ref, out_ref, *, gate_idx, o_transposed):
    y = _dot_tn(o_ref[...], w_ref[...]) if o_transposed else _dot(o_ref[...], w_ref[...])
    out_ref[...] = h_ref[...] + mod_ref[gate_idx:gate_idx + 1, :] * _rms(y, g_ref[...])


def out_proj_residual(o, w, h, g, mod, *, gate_idx, rows_per_mod, mod_base, o_transposed=False, tm=512):
    k, d = w.shape
    rows = h.shape[0]
    tm = min(tm, rows)
    o_spec = pl.BlockSpec((k, tm), lambda i: (0, i)) if o_transposed else pl.BlockSpec((tm, k), lambda i: (i, 0))
    return pl.pallas_call(
        functools.partial(_out_proj_kernel, gate_idx=gate_idx, o_transposed=o_transposed),
        grid=(rows // tm,),
        in_specs=[
            o_spec,
            pl.BlockSpec((k, d), lambda i: (0, 0)),
            pl.BlockSpec((tm, d), lambda i: (i, 0)),
            pl.BlockSpec((1, d), lambda i: (0, 0)),
            pl.BlockSpec((None, N_MOD, d), _mod_index_map(tm, rows_per_mod, mod_base)),
        ],
        out_specs=pl.BlockSpec((tm, d), lambda i: (i, 0)),
        out_shape=jax.ShapeDtypeStruct((rows, d), F32),
        compiler_params=_cparams("parallel"),
        name="out_proj_residual",
    )(o, w, h, g.reshape(1, d), mod)


def _ffn_kernel(x_ref, g1_ref, mod_ref, w1_ref, w2_ref, g2_ref, out_ref, a_scr,
                *, shift_idx, scale_idx, gate_idx):
    f = pl.program_id(1)

    @pl.when(f == 0)
    def _():
        y = _rms(x_ref[...], g1_ref[...])
        y = y * (1.0 + mod_ref[scale_idx:scale_idx + 1, :]) + mod_ref[shift_idx:shift_idx + 1, :]
        a_scr[...] = y.astype(BF16)
        out_ref[...] = jnp.zeros_like(out_ref)

    hid = jnp.maximum(_dot(a_scr[...], w1_ref[...].astype(BF16)), 0.0)
    out_ref[...] += _dot((hid * hid).astype(BF16), w2_ref[...].astype(BF16))

    @pl.when(f == pl.num_programs(1) - 1)
    def _():
        out_ref[...] = x_ref[...] + mod_ref[gate_idx:gate_idx + 1, :] * _rms(out_ref[...], g2_ref[...])


def ffn_residual(x, g_pre, g_post, mod, w1_all, w2_all, layer, *, rows_per_mod, mod_base, tm=1024, tf=512):
    rows, d = x.shape
    dff = w1_all.shape[2]
    tm = min(tm, rows)
    return pl.pallas_call(
        functools.partial(_ffn_kernel, shift_idx=3, scale_idx=4, gate_idx=5),
        grid=(rows // tm, dff // tf),
        in_specs=[
            pl.BlockSpec((tm, d), lambda i, f: (i, 0), pipeline_mode=pl.Buffered(1)),
            pl.BlockSpec((1, d), lambda i, f: (0, 0)),
            pl.BlockSpec((None, N_MOD, d), _mod_index_map(tm, rows_per_mod, mod_base)),
            pl.BlockSpec((None, d, tf), lambda i, f: (layer, 0, f)),
            pl.BlockSpec((None, tf, d), lambda i, f: (layer, f, 0)),
            pl.BlockSpec((1, d), lambda i, f: (0, 0)),
        ],
        out_specs=pl.BlockSpec((tm, d), lambda i, f: (i, 0)),
        out_shape=jax.ShapeDtypeStruct((rows, d), F32),
        scratch_shapes=[pltpu.VMEM((tm, d), BF16)],
        compiler_params=_cparams("parallel", "arbitrary"),
        name="ffn_residual",
    )(x, g_pre.reshape(1, d), mod, w1_all, w2_all, g_post.reshape(1, d))


def _ctx_attn_kernel(q_ref, k_ref, v_ref, o_ref, *, scale, base2):
    s = _dot_nt(q_ref[...], k_ref[...])
    if scale != 1.0:
        s = s * scale
    m = jnp.max(s, axis=-1, keepdims=True)
    p = jnp.exp2(s - m) if base2 else jnp.exp(s - m)
    l = jnp.sum(p, axis=-1, keepdims=True)
    o_ref[...] = (_dot(p.astype(BF16), v_ref[...]) / l).astype(o_ref.dtype)


def ctx_attention(q_arr, k_arr, v_arr, *, batch, heads, lc, dq, dv, q_off, k_off, v_off, scale, base2=False):
    return pl.pallas_call(
        functools.partial(_ctx_attn_kernel, scale=scale, base2=base2),
        grid=(batch, heads),
        in_specs=[
            pl.BlockSpec((lc, dq), lambda b, h: (b, q_off + h)),
            pl.BlockSpec((lc, dq), lambda b, h: (b, k_off + h)),
            pl.BlockSpec((lc, dv), lambda b, h: (b, v_off + h)),
        ],
        out_specs=pl.BlockSpec((lc, dv), lambda b, h: (b, h)),
        out_shape=jax.ShapeDtypeStruct((batch * lc, heads * dv), BF16),
        compiler_params=_cparams("parallel", "parallel"),
        name="ctx_attention",
    )(q_arr, k_arr, v_arr)


def _na_window_tables(img_rows):
    kr = min(NA_WIN_R, img_rows)
    n_blk = img_rows // NA_Q_ROWS

    def tables(i):
        ks = int(np.clip(i * NA_Q_ROWS - NA_WIN_R // 2, 0, img_rows - NA_K_ROWS))
        r = i * NA_Q_ROWS + np.arange(NA_Q_ROWS)[:, None]
        kr_abs = ks + np.arange(NA_K_ROWS)[None, :]
        rs = np.clip(r - kr // 2, 0, img_rows - kr)
        valid = (kr_abs >= rs) & (kr_abs < rs + kr)
        dr = np.clip(kr_abs - r + NA_WIN_R - 1, 0, 2 * NA_WIN_R - 2)
        return ks, valid, dr

    first, inner, last = tables(0), tables(1), tables(n_blk - 1)
    for i in range(1, n_blk - 1):
        ks, valid, dr = tables(i)
        assert ks == i * NA_Q_ROWS - NA_WIN_R // 2
        assert (valid == inner[1]).all() and (dr[valid] == inner[2][valid]).all()
    assert first[0] == 0 and last[0] == img_rows - NA_K_ROWS
    return [first, inner, last]


def _na_bias_tables(rpb, img_rows):
    qcol = np.arange(GRID_W)
    col_start = np.clip(qcol - NA_WIN_C // 2, 0, GRID_W - NA_WIN_C)
    col_mask = (qcol[None, :] >= col_start[:, None]) & (qcol[None, :] < col_start[:, None] + NA_WIN_C)
    dc_idx = np.clip(qcol[None, :] - qcol[:, None], -(NA_WIN_C - 1), NA_WIN_C - 1) + NA_WIN_C - 1
    rpb_cols = rpb[:, :, dc_idx]
    out = []
    for _, valid, dr in _na_window_tables(img_rows):
        t = rpb_cols[:, dr]
        mask = valid[:, :, None, None] & col_mask[None, None]
        t = jnp.where(mask[None], t * math.log2(math.e), NEG_BIG)
        t = t.transpose(0, 2, 4, 1, 3).reshape(rpb.shape[0], NA_K_ROWS * GRID_W, NA_Q_ROWS * GRID_W)
        out.append(t)
    return jnp.stack(out, axis=1)


def _na_kernel(q_ref, k_ref, vt_ref, kc_ref, vct_ref, bias_ref, ot_ref, *, img_rows):
    qb = NA_Q_ROWS * GRID_W
    kb = NA_K_ROWS * GRID_W
    n_blk = img_rows // NA_Q_ROWS
    kc = kc_ref[...]
    vct = vct_ref[...]

    def offsets(i):
        if isinstance(i, int):
            return i * qb, min(max(i * NA_Q_ROWS - NA_WIN_R // 2, 0), img_rows - NA_K_ROWS) * GRID_W
        q0 = pl.multiple_of(i * qb, qb)
        return q0, pl.multiple_of(q0 - (NA_WIN_R // 2) * GRID_W, qb)

    def logits(i, variant):
        q0, ks = offsets(i)
        q = q_ref[pl.ds(q0, qb), :]
        return _dot_nt(k_ref[pl.ds(ks, kb), :], q) + bias_ref[variant], _dot_nt(kc, q)

    def softmax(s_w, s_c):
        m = jnp.maximum(jnp.max(s_w, axis=0, keepdims=True), jnp.max(s_c, axis=0, keepdims=True))
        p_w = jnp.exp2(s_w - m)
        p_c = jnp.exp2(s_c - m)
        l = jnp.sum(p_w, axis=0, keepdims=True) + jnp.sum(p_c, axis=0, keepdims=True)
        return p_w.astype(BF16), p_c.astype(BF16), l

    def weighted_values(i, p_w, p_c, l):
        q0, ks = offsets(i)
        o = _dot(vt_ref[:, pl.ds(ks, kb)], p_w) + _dot(vct, p_c)
        ot_ref[:, pl.ds(q0, qb)] = (o / l).astype(ot_ref.dtype)

    def group(blocks):
        n = len(blocks)
        s, p = {}, {}
        for step in range(n + 2):
            if step < n:
                s[step] = logits(*blocks[step])
            if 1 <= step <= n:
                p[step - 1] = softmax(*s.pop(step - 1))
            if 2 <= step:
                weighted_values(blocks[step - 2][0], *p.pop(step - 2))

    group([(0, 0), (n_blk - 1, 2)])
    n_inner = n_blk - 2
    per_iter = next(g for g in (5, 4, 3, 2, 1) if n_inner % g == 0)

    def body(it, carry):
        group([(1 + it * per_iter + j, 1) for j in range(per_iter)])
        return carry

    lax.fori_loop(0, n_inner // per_iter, body, 0)


def na_attention(q_lat, k_lat, vt_lat, k_ctx, vt_ctx, bias, *, batch, seq, lc):
    heads = NA_HEADS
    dh = q_lat.shape[1] // heads
    img_rows = seq // GRID_W
    qb = NA_Q_ROWS * GRID_W
    kb = NA_K_ROWS * GRID_W
    assert (NA_WIN_R // 2) * GRID_W == qb
    return pl.pallas_call(
        functools.partial(_na_kernel, img_rows=img_rows),
        grid=(batch, heads),
        in_specs=[
            pl.BlockSpec((seq, dh), lambda b, h: (b, h)),
            pl.BlockSpec((seq, dh), lambda b, h: (b, h)),
            pl.BlockSpec((dh, seq), lambda b, h: (h, b)),
            pl.BlockSpec((lc, dh), lambda b, h: (b, h)),
            pl.BlockSpec((dh, lc), lambda b, h: (h, b)),
            pl.BlockSpec((None, 3, kb, qb), lambda b, h: (h, 0, 0, 0)),
        ],
        out_specs=pl.BlockSpec((dh, seq), lambda b, h: (h, b)),
        out_shape=jax.ShapeDtypeStruct((heads * dh, batch * seq), BF16),
        compiler_params=_cparams("parallel", "parallel"),
        name="na_attention",
    )(q_lat, k_lat, vt_lat, k_ctx, vt_ctx, bias)


def _mla_up_kernel(z_ref, gq_ref, gkv_ref, wq_ref, wkv_ref, cos_ref, sin_ref,
                   q_ref, k_ref, vt_ref, *maybe_v_ref, scale):
    hq = MLA_HEADS * MLA_QK_PAD
    z = z_ref[...]
    cq = _rms(z[:, :MLA_Q_RANK], gq_ref[...]).astype(BF16)
    ckv = _rms(z[:, MLA_Q_RANK:MLA_Q_RANK + MLA_KV_RANK], gkv_ref[...]).astype(BF16)
    cos = cos_ref[...]
    sin = sin_ref[...]
    c0 = MLA_Q_RANK + MLA_KV_RANK
    kpe = (z[:, c0:c0 + LANE] * cos + z[:, c0 + LANE:c0 + 2 * LANE] * sin).astype(BF16)
    qraw = _dot(cq, wq_ref[...])
    kv = _dot(ckv, wkv_ref[...])
    for h in range(MLA_HEADS):
        a = h * MLA_QK_PAD
        q_ref[:, a:a + LANE] = (qraw[:, a:a + LANE] * scale).astype(BF16)
        pe = qraw[:, a + LANE:a + 2 * LANE]
        sw = qraw[:, hq + h * LANE:hq + (h + 1) * LANE]
        q_ref[:, a + LANE:a + 2 * LANE] = ((pe * cos + sw * sin) * scale).astype(BF16)
        k_ref[:, a:a + LANE] = kv[:, h * MLA_NOPE:(h + 1) * MLA_NOPE].astype(BF16)
        k_ref[:, a + LANE:a + 2 * LANE] = kpe
    v = kv[:, MLA_HEADS * MLA_NOPE:]
    vt_ref[...] = v.T.astype(BF16)
    for v_ref in maybe_v_ref:
        v_ref[...] = v.astype(BF16)


def mla_up(z, g_q, g_kv, wq, wkv, cos, sin, *, with_v, tm=256):
    rows = z.shape[0]
    tm = min(tm, rows)
    n_pos = cos.shape[0] // tm
    hq = MLA_HEADS * MLA_QK_PAD
    hv = MLA_HEADS * MLA_V
    v_specs = [pl.BlockSpec((tm, hv), lambda i: (i, 0))] if with_v else []
    v_shapes = [jax.ShapeDtypeStruct((rows, hv), BF16)] if with_v else []
    return pl.pallas_call(
        functools.partial(_mla_up_kernel, scale=(MLA_NOPE + MLA_ROPE) ** -0.5 * math.log2(math.e)),
        grid=(rows // tm,),
        in_specs=[
            pl.BlockSpec((tm, z.shape[1]), lambda i: (i, 0)),
            pl.BlockSpec((1, MLA_Q_RANK), lambda i: (0, 0)),
            pl.BlockSpec((1, MLA_KV_RANK), lambda i: (0, 0)),
            pl.BlockSpec(wq.shape, lambda i: (0, 0)),
            pl.BlockSpec(wkv.shape, lambda i: (0, 0)),
            pl.BlockSpec((tm, LANE), lambda i: (i % n_pos, 0)),
            pl.BlockSpec((tm, LANE), lambda i: (i % n_pos, 0)),
        ],
        out_specs=[
            pl.BlockSpec((tm, hq), lambda i: (i, 0)),
            pl.BlockSpec((tm, hq), lambda i: (i, 0)),
            pl.BlockSpec((hv, tm), lambda i: (0, i)),
        ] + v_specs,
        out_shape=[
            jax.ShapeDtypeStruct((rows, hq), BF16),
            jax.ShapeDtypeStruct((rows, hq), BF16),
            jax.ShapeDtypeStruct((hv, rows), BF16),
        ] + v_shapes,
        compiler_params=_cparams("parallel"),
        name="mla_up",
    )(z, g_q.reshape(1, -1), g_kv.reshape(1, -1), wq, wkv, cos, sin)


def _mla_attn_kernel(q_ref, k_ref, vt_ref, kc_ref, vct_ref, ot_ref, *, tk):
    q = q_ref[...]

    def logits(c):
        return _dot_nt(k_ref[c * tk:(c + 1) * tk, :], q)

    def tile(st, vt, state):
        m_tile = jnp.max(st, axis=0, keepdims=True)
        if state is None:
            m_new = m_tile
        else:
            m, l, acc = state
            m_new = jnp.maximum(m, m_tile)
            alpha = jnp.exp2(m - m_new)
        p = jnp.exp2(st - m_new)
        p_sum = jnp.sum(p, axis=0, keepdims=True)
        pv = _dot(vt, p.astype(BF16))
        if state is None:
            return m_new, p_sum, pv
        return m_new, alpha * l + p_sum, alpha * acc + pv

    n_kv = k_ref.shape[0] // tk
    st_next = logits(0)
    state = tile(_dot_nt(kc_ref[...], q), vct_ref[...], None)
    for c in range(n_kv):
        st = st_next
        if c + 1 < n_kv:
            st_next = logits(c + 1)
        state = tile(st, vt_ref[:, c * tk:(c + 1) * tk], state)
    _, l, acc = state
    ot_ref[...] = (acc / l).astype(ot_ref.dtype)


def mla_attention(q_lat, k_lat, vt_lat, k_ctx, vt_ctx, *, batch, seq, lc, tq=512, tk=1024):
    heads = MLA_HEADS
    tq = min(tq, seq)
    tk = min(tk, seq)
    nq = seq // tq
    return pl.pallas_call(
        functools.partial(_mla_attn_kernel, tk=tk),
        grid=(batch, heads, nq),
        in_specs=[
            pl.BlockSpec((tq, MLA_QK_PAD), lambda b, h, i: (b * nq + i, h)),
            pl.BlockSpec((seq, MLA_QK_PAD), lambda b, h, i: (b, h)),
            pl.BlockSpec((MLA_V, seq), lambda b, h, i: (h, b)),
            pl.BlockSpec((lc, MLA_QK_PAD), lambda b, h, i: (b, h)),
            pl.BlockSpec((MLA_V, lc), lambda b, h, i: (h, b)),
        ],
        out_specs=pl.BlockSpec((MLA_V, tq), lambda b, h, i: (h, b * nq + i)),
        out_shape=jax.ShapeDtypeStruct((heads * MLA_V, batch * seq), BF16),
        compiler_params=_cparams("parallel", "parallel", "arbitrary"),
        name="mla_attention",
    )(q_lat, k_lat, vt_lat, k_ctx, vt_ctx)


def _rope_tables(seq):
    t = np.arange(seq)
    pos = np.stack([t // GRID_W, t % GRID_W], axis=-1).astype(np.float32)
    freqs = jnp.asarray(ROPE_BASE, F32) ** (-jnp.arange(ROPE_PAIRS, dtype=F32) / ROPE_PAIRS)
    ang = jnp.asarray(pos)[:, :, None] * freqs
    cos, sin = jnp.cos(ang), jnp.sin(ang)
    cos_t = jnp.stack([cos, cos], axis=2).reshape(seq, MLA_ROPE)
    sin_t = jnp.stack([-sin, sin], axis=2).reshape(seq, MLA_ROPE)
    pad = jnp.zeros((seq, LANE - MLA_ROPE), F32)
    return jnp.concatenate([cos_t, pad], axis=1), jnp.concatenate([sin_t, pad], axis=1)


_ROPE_SWAP = np.arange(MLA_ROPE).reshape(2, 2, ROPE_PAIRS)[:, ::-1, :].reshape(-1)


def _mla_weights(w_in, w_uq, w_ukv):
    d = w_in.shape[0]
    c0 = MLA_Q_RANK + MLA_KV_RANK
    zpad = jnp.zeros((d, LANE - MLA_ROPE), w_in.dtype)
    kpe = w_in[:, c0:]
    w_in_x = jnp.concatenate([w_in[:, :c0], kpe, zpad, kpe[:, _ROPE_SWAP], zpad], axis=1)
    r = w_uq.shape[0]
    wq = w_uq.reshape(r, MLA_HEADS, MLA_NOPE + MLA_ROPE)
    zq = jnp.zeros((r, MLA_HEADS, LANE - MLA_ROPE), w_uq.dtype)
    main = jnp.concatenate([wq, zq], axis=2).reshape(r, MLA_HEADS * MLA_QK_PAD)
    swapped = jnp.concatenate([wq[:, :, MLA_NOPE:][:, :, _ROPE_SWAP], zq], axis=2).reshape(r, MLA_HEADS * LANE)
    wq_x = jnp.concatenate([main, swapped], axis=1)
    wkv = w_ukv.reshape(w_ukv.shape[0], MLA_HEADS, MLA_NOPE + MLA_V)
    wkv_x = jnp.concatenate([wkv[:, :, :MLA_NOPE].reshape(r, -1), wkv[:, :, MLA_NOPE:].reshape(r, -1)], axis=1)
    return w_in_x.astype(BF16), wq_x.astype(BF16), wkv_x.astype(BF16)


def _ml_conv_kernel(prev_ref, x_ref, next_ref, w_ref, q_ref, k_ref, *, seg_len, scale):
    tm = x_ref.shape[0]
    hk = ML_HEADS * ML_QK
    x = x_ref[...]
    row = lax.broadcasted_iota(jnp.int32, (tm, 1), 0)
    pos = (pl.program_id(0) * tm) % seg_len + row
    x_prev = jnp.where(row == 0, prev_ref[7:8, :], pltpu.roll(x, 1, 0))
    x_next = jnp.where(row == tm - 1, next_ref[0:1, :], pltpu.roll(x, tm - 1, 0))
    x_prev = jnp.where(pos == 0, 0.0, x_prev)
    x_next = jnp.where(pos == seg_len - 1, 0.0, x_next)
    y = x_prev * w_ref[0:1, :] + x * w_ref[1:2, :] + x_next * w_ref[2:3, :]
    y = y * jax.nn.sigmoid(y)
    q_ref[...] = (y[:, :hk] * scale).astype(BF16)
    k_ref[...] = y[:, hk:].astype(BF16)


def ml_conv_silu(qk_raw, conv_w, *, seg_len, tm=256):
    rows, c = qk_raw.shape
    tm = min(tm, seg_len)
    hk = ML_HEADS * ML_QK
    nb8 = rows // 8
    t8 = tm // 8
    return pl.pallas_call(
        functools.partial(_ml_conv_kernel, seg_len=seg_len, scale=ML_QK ** -0.5),
        grid=(rows // tm,),
        in_specs=[
            pl.BlockSpec((8, c), lambda i: (jnp.maximum(i * t8 - 1, 0), 0)),
            pl.BlockSpec((tm, c), lambda i: (i, 0)),
            pl.BlockSpec((8, c), lambda i: (jnp.minimum((i + 1) * t8, nb8 - 1), 0)),
            pl.BlockSpec((ML_CONV, c), lambda i: (0, 0)),
        ],
        out_specs=[pl.BlockSpec((tm, hk), lambda i: (i, 0)), pl.BlockSpec((tm, hk), lambda i: (i, 0))],
        out_shape=[jax.ShapeDtypeStruct((rows, hk), BF16), jax.ShapeDtypeStruct((rows, hk), BF16)],
        compiler_params=_cparams("parallel"),
        name="ml_conv_silu",
    )(qk_raw, qk_raw, qk_raw, conv_w)


def _log_sigmoid(x):
    return jnp.minimum(x, 0.0) - jnp.log(1.0 + jnp.exp(-jnp.abs(x)))


def _ml_scan_kernel(q_ref, k_ref, v_ref, g_ref, gt_ref, b_ref, bt_ref, c0_ref, n0_ref, m0_ref,
                    h_ref, c_out, n_out, m_out, c_scr, n_scr, m_scr, *, reverse):
    step = pl.program_id(1)
    tc = q_ref.shape[0]

    @pl.when(step == 0)
    def _():
        c_scr[...] = c0_ref[...]
        n_scr[...] = n0_ref[...]
        m_scr[...] = m0_ref[...]

    t_idx = lax.broadcasted_iota(jnp.int32, (tc, tc), 0)
    s_idx = lax.broadcasted_iota(jnp.int32, (tc, tc), 1)
    seen = (s_idx >= t_idx) if reverse else (s_idx <= t_idx)
    seen_t = (t_idx >= s_idx) if reverse else (t_idx <= s_idx)
    gi = 2 * ML_HEADS if reverse else 0
    gf = gi + ML_HEADS

    g = g_ref[...] + b_ref[...]
    gt = gt_ref[...] + bt_ref[...]

    for hd in range(ML_HEADS):
        qh = q_ref[:, hd * ML_QK:(hd + 1) * ML_QK]
        kh = k_ref[:, hd * ML_QK:(hd + 1) * ML_QK]
        vh = v_ref[:, hd * ML_V:(hd + 1) * ML_V]
        li_c = g[:, gi + hd:gi + hd + 1]
        lf_c = _log_sigmoid(g[:, gf + hd:gf + hd + 1])
        li_r = gt[gi + hd:gi + hd + 1, :]
        lf_r = _log_sigmoid(gt[gf + hd:gf + hd + 1, :])
        b_c = jnp.sum(jnp.where(seen, lf_r, 0.0), axis=1, keepdims=True)
        b_r = jnp.sum(jnp.where(seen_t, lf_c, 0.0), axis=0, keepdims=True)
        b_end = jnp.sum(lf_r, axis=1, keepdims=True)
        m_prev = m_scr[hd:hd + 1, 0:1]
        c_prev = c_scr[hd]
        n_prev = n_scr[hd:hd + 1, :]

        log_w = jnp.where(seen, b_c - b_r + li_r, NEG_BIG)
        log_inter = b_c + m_prev
        m_t = jnp.maximum(log_inter, jnp.max(log_w, axis=1, keepdims=True))
        w_inter = jnp.exp(log_inter - m_t)
        s = _dot_nt(qh, kh) * jnp.exp(log_w - m_t)
        num = w_inter * _dot(qh, c_prev.astype(BF16)) + _dot(s.astype(BF16), vh)
        den = (w_inter * jnp.sum(qh.astype(F32) * n_prev, axis=1, keepdims=True)
               + jnp.sum(s, axis=1, keepdims=True))
        h_ref[:, hd * ML_V:(hd + 1) * ML_V] = num / jnp.maximum(jnp.abs(den), jnp.exp(-m_t))

        log_s = b_end - b_c + li_c
        m_new = jnp.maximum(b_end + m_prev, jnp.max(log_s, axis=0, keepdims=True))
        decay = jnp.exp(b_end + m_prev - m_new)
        kw = kh.astype(F32) * jnp.exp(log_s - m_new)
        c_scr[hd] = decay * c_prev + _dot_tn(kw.astype(BF16), vh)
        n_scr[hd:hd + 1, :] = decay * n_prev + jnp.sum(kw, axis=0, keepdims=True)
        m_scr[hd:hd + 1, :] = jnp.broadcast_to(m_new, (1, LANE))

    @pl.when(step == pl.num_programs(1) - 1)
    def _():
        c_out[...] = c_scr[...]
        n_out[...] = n_scr[...]
        m_out[...] = m_scr[...]


def ml_scan(q, k, v, g, gt, b_row, b_col, state, *, batch, seg_len, reverse):
    tc = min(ML_CHUNK, seg_len)
    nc = seg_len // tc
    hk = ML_HEADS * ML_QK
    hv = ML_HEADS * ML_V
    c0, n0, m0 = state

    def blk(b, s):
        return b * nc + ((nc - 1 - s) if reverse else s)

    state_specs = [
        pl.BlockSpec((None, ML_HEADS, ML_QK, ML_V), lambda b, s: (b, 0, 0, 0)),
        pl.BlockSpec((None, ML_HEADS, ML_QK), lambda b, s: (b, 0, 0)),
        pl.BlockSpec((None, ML_HEADS, LANE), lambda b, s: (b, 0, 0)),
    ]
    return pl.pallas_call(
        functools.partial(_ml_scan_kernel, reverse=reverse),
        grid=(batch, nc),
        in_specs=[
            pl.BlockSpec((tc, hk), lambda b, s: (blk(b, s), 0)),
            pl.BlockSpec((tc, hk), lambda b, s: (blk(b, s), 0)),
            pl.BlockSpec((tc, hv), lambda b, s: (blk(b, s), 0)),
            pl.BlockSpec((tc, LANE), lambda b, s: (blk(b, s), 0)),
            pl.BlockSpec((4 * ML_HEADS, tc), lambda b, s: (0, blk(b, s))),
            pl.BlockSpec((1, LANE), lambda b, s: (0, 0)),
            pl.BlockSpec((4 * ML_HEADS, 1), lambda b, s: (0, 0)),
        ] + state_specs,
        out_specs=[pl.BlockSpec((tc, hv), lambda b, s: (blk(b, s), 0))] + state_specs,
        out_shape=[
            jax.ShapeDtypeStruct((batch * seg_len, hv), F32),
            jax.ShapeDtypeStruct(c0.shape, F32),
            jax.ShapeDtypeStruct(n0.shape, F32),
            jax.ShapeDtypeStruct(m0.shape, F32),
        ],
        scratch_shapes=[
            pltpu.VMEM((ML_HEADS, ML_QK, ML_V), F32),
            pltpu.VMEM((ML_HEADS, ML_QK), F32),
            pltpu.VMEM((ML_HEADS, LANE), F32),
        ],
        compiler_params=_cparams("parallel", "arbitrary"),
        name="ml_scan_bwd" if reverse else "ml_scan_fwd",
    )(q, k, v, g, gt, b_row, b_col, c0, n0, m0)


def _ml_out_kernel(hf_ref, hb_ref, og_ref, gh_ref, w_ref, h_ref, g_ref, mod_ref, out_ref, x_scr, *, gate_idx):
    og = og_ref[...]
    sig = jax.nn.sigmoid(og)
    for hd in range(ML_HEADS):
        sl = slice(hd * ML_V, (hd + 1) * ML_V)
        hh = hf_ref[:, sl] + hb_ref[:, sl]
        x_scr[:, sl] = (sig[:, sl] * _rms(hh, gh_ref[:, sl])).astype(BF16)
    y = _dot(x_scr[...], w_ref[...])
    out_ref[...] = h_ref[...] + mod_ref[gate_idx:gate_idx + 1, :] * _rms(y, g_ref[...])


def ml_out_residual(h_f, h_b, og, g_head, w, h, g, mod, *, gate_idx, rows_per_mod, mod_base, tm=256):
    rows, hv = h_f.shape
    d = w.shape[1]
    tm = min(tm, rows)
    return pl.pallas_call(
        functools.partial(_ml_out_kernel, gate_idx=gate_idx),
        grid=(rows // tm,),
        in_specs=[
            pl.BlockSpec((tm, hv), lambda i: (i, 0)),
            pl.BlockSpec((tm, hv), lambda i: (i, 0)),
            pl.BlockSpec((tm, hv), lambda i: (i, 0)),
            pl.BlockSpec((1, hv), lambda i: (0, 0)),
            pl.BlockSpec((hv, d), lambda i: (0, 0)),
            pl.BlockSpec((tm, d), lambda i: (i, 0)),
            pl.BlockSpec((1, d), lambda i: (0, 0)),
            pl.BlockSpec((None, N_MOD, d), _mod_index_map(tm, rows_per_mod, mod_base)),
        ],
        out_specs=pl.BlockSpec((tm, d), lambda i: (i, 0)),
        out_shape=jax.ShapeDtypeStruct((rows, d), F32),
        scratch_shapes=[pltpu.VMEM((tm, hv), BF16)],
        compiler_params=_cparams("parallel"),
        name="ml_out_residual",
    )(h_f, h_b, og, g_head.reshape(1, hv), w, h, g.reshape(1, d), mod)


def _streams(batch, seq, lc):
    return dict(rows_per_mod=seq, mod_base=0), dict(rows_per_mod=batch * lc, mod_base=batch)


def na_mixer(h_lat, h_ctx, mod, g_pre, g_post, w_qkv, rpb, w_o, *, batch, seq, lc, need_ctx):
    lat, ctx = _streams(batch, seq, lc)
    hd = w_o.shape[0]
    dh = hd // NA_HEADS
    w = w_qkv.astype(BF16)
    wo = w_o.astype(BF16)
    q_out = ProjOut(w[:, :hd], BF16, scale=dh ** -0.5 * math.log2(math.e))
    k_out = ProjOut(w[:, hd:2 * hd], BF16)
    v_out = ProjOut(w[:, 2 * hd:], BF16)
    vt_out = v_out._replace(transposed=True)
    proj = functools.partial(norm_proj, g=g_pre, mod=mod, shift_idx=0, scale_idx=1, tn=512)
    q_lat, k_lat, vt_lat = proj(h_lat, outs=[q_out, k_out, vt_out], tm=1024, **lat)
    ctx_outs = [q_out, k_out, vt_out] + ([v_out] if need_ctx else [])
    q_ctx, k_ctx, vt_ctx, *v_ctx = proj(h_ctx, outs=ctx_outs, tm=512, **ctx)
    bias = _na_bias_tables(rpb, seq // GRID_W)
    ot_lat = na_attention(q_lat, k_lat, vt_lat, k_ctx, vt_ctx, bias, batch=batch, seq=seq, lc=lc)
    h_lat = out_proj_residual(ot_lat, wo, h_lat, g_post, mod, gate_idx=2, o_transposed=True, **lat)
    if need_ctx:
        o_ctx = ctx_attention(q_ctx, k_ctx, v_ctx[0], batch=batch, heads=NA_HEADS, lc=lc, dq=dh, dv=dh,
                              q_off=0, k_off=0, v_off=0, scale=1.0, base2=True)
        h_ctx = out_proj_residual(o_ctx, wo, h_ctx, g_post, mod, gate_idx=2, **ctx)
    return h_lat, h_ctx


def mla_mixer(h_lat, h_ctx, mod, g_pre, g_post, w_in, g_q, g_kv, w_uq, w_ukv, w_o, *, batch, seq, lc, need_ctx):
    lat, ctx = _streams(batch, seq, lc)
    w_in_x, wq_x, wkv_x = _mla_weights(w_in, w_uq, w_ukv)
    wo = w_o.astype(BF16)
    proj = functools.partial(norm_proj, g=g_pre, outs=[ProjOut(w_in_x, F32)], mod=mod, shift_idx=0, scale_idx=1,
                             tn=w_in_x.shape[1])
    z_lat, = proj(h_lat, tm=1024, **lat)
    z_ctx, = proj(h_ctx, tm=512, **ctx)
    cos_l, sin_l = _rope_tables(seq)
    ones = jnp.concatenate([jnp.ones((lc, MLA_ROPE), F32), jnp.zeros((lc, LANE - MLA_ROPE), F32)], axis=1)
    q_lat, k_lat, vt_lat = mla_up(z_lat, g_q, g_kv, wq_x, wkv_x, cos_l, sin_l, with_v=False)
    q_ctx, k_ctx, vt_ctx, v_ctx = mla_up(z_ctx, g_q, g_kv, wq_x, wkv_x, ones, jnp.zeros_like(ones), with_v=True)
    ot_lat = mla_attention(q_lat, k_lat, vt_lat, k_ctx, vt_ctx, batch=batch, seq=seq, lc=lc)
    h_lat = out_proj_residual(ot_lat, wo, h_lat, g_post, mod, gate_idx=2, o_transposed=True, **lat)
    if need_ctx:
        o_ctx = ctx_attention(q_ctx, k_ctx, v_ctx, batch=batch, heads=MLA_HEADS, lc=lc, dq=MLA_QK_PAD, dv=MLA_V,
                              q_off=0, k_off=0, v_off=0, scale=1.0, base2=True)
        h_ctx = out_proj_residual(o_ctx, wo, h_ctx, g_post, mod, gate_idx=2, **ctx)
    return h_lat, h_ctx


def mlstm_mixer(h_lat, h_ctx, mod, g_pre, g_post, w_in, b_gate, conv_w, g_head, w_o, *, batch, seq, lc, need_ctx):
    lat, ctx = _streams(batch, seq, lc)
    d = w_in.shape[0]
    hk2 = 2 * ML_HEADS * ML_QK
    hv = ML_HEADS * ML_V
    ng = 4 * ML_HEADS
    w_qk = w_in[:, :hk2].astype(BF16)
    w_v = w_in[:, hk2:hk2 + hv].astype(BF16)
    w_og = w_in[:, hk2 + hv:hk2 + 2 * hv].astype(BF16)
    w_g = jnp.concatenate([w_in[:, hk2 + 2 * hv:], jnp.zeros((d, LANE - ng), w_in.dtype)], axis=1).astype(BF16)
    wo = w_o.astype(BF16)
    b_row = jnp.concatenate([b_gate, jnp.zeros((LANE - ng,), F32)]).reshape(1, LANE)
    b_col = b_gate.reshape(ng, 1)

    def project(h, stream, seg_len):
        qk_raw, v, og, g = norm_proj(
            h, g_pre, [ProjOut(w_qk, F32), ProjOut(w_v, BF16), ProjOut(w_og, F32)], narrow=ProjOut(w_g, F32),
            mod=mod, shift_idx=0, scale_idx=1, tm=512, tn=512, **stream)
        q, k = ml_conv_silu(qk_raw, conv_w, seg_len=seg_len)
        return q, k, v, og, g, g[:, :ng].T

    def bidir(q, k, v, g, gt, st_f, st_b, seg_len):
        h_f, *st_f = ml_scan(q, k, v, g, gt, b_row, b_col, st_f, batch=batch, seg_len=seg_len, reverse=False)
        h_b, *st_b = ml_scan(q, k, v, g, gt, b_row, b_col, st_b, batch=batch, seg_len=seg_len, reverse=True)
        return h_f, h_b, st_f, st_b

    zero = (jnp.zeros((batch, ML_HEADS, ML_QK, ML_V), F32), jnp.zeros((batch, ML_HEADS, ML_QK), F32),
            jnp.zeros((batch, ML_HEADS, LANE), F32))
    qc, kc, vc, ogc, gc, gtc = project(h_ctx, ctx, lc)
    ql, kl, vl, ogl, gl, gtl = project(h_lat, lat, seq)
    hc_f, hc_b, st_f, st_b = bidir(qc, kc, vc, gc, gtc, zero, zero, lc)
    hl_f, hl_b, _, _ = bidir(ql, kl, vl, gl, gtl, st_f, st_b, seq)
    h_lat = ml_out_residual(hl_f, hl_b, ogl, g_head, wo, h_lat, g_post, mod, gate_idx=2, **lat)
    if need_ctx:
        h_ctx = ml_out_residual(hc_f, hc_b, ogc, g_head, wo, h_ctx, g_post, mod, gate_idx=2, **ctx)
    return h_lat, h_ctx


def kernel(x, c, ctx, c_ctx, ada_w, ada_b, norm_g, ff_w1, ff_w2, na_w_qkv, na_rpb, na_w_o, mla_w_in, mla_g_q, mla_g_kv, mla_w_uq, mla_w_ukv, mla_w_o, ml_w_in, ml_b_gate, ml_conv, ml_g_head, ml_w_o):
    batch, seq, d = x.shape
    lc = ctx.shape[1]
    depth = ada_w.shape[0]
    lat, cst = _streams(batch, seq, lc)

    n_cond = 8 * ((batch + 1 + 7) // 8)
    c_all = jnp.concatenate([c, c_ctx[None, :], jnp.zeros((n_cond - batch - 1, d), F32)], axis=0)
    mods = ada_modulation(c_all, ada_w, ada_b).reshape(depth, n_cond, N_MOD, d)

    h_lat = x.reshape(batch * seq, d)
    h_ctx = ctx.reshape(batch * lc, d)
    for i in range(depth):
        last = i == depth - 1
        mod = mods[i]
        g_pre1, g_post1, g_pre2, g_post2 = norm_g[i]
        kind, j = i % 3, i // 3
        dims = dict(batch=batch, seq=seq, lc=lc, need_ctx=not last)
        if kind == 0:
            h_lat, h_ctx = na_mixer(h_lat, h_ctx, mod, g_pre1, g_post1, na_w_qkv[j], na_rpb[j], na_w_o[j], **dims)
        elif kind == 1:
            h_lat, h_ctx = mla_mixer(h_lat, h_ctx, mod, g_pre1, g_post1, mla_w_in[j], mla_g_q[j], mla_g_kv[j],
                                     mla_w_uq[j], mla_w_ukv[j], mla_w_o[j], **dims)
        else:
            h_lat, h_ctx = mlstm_mixer(h_lat, h_ctx, mod, g_pre1, g_post1, ml_w_in[j], ml_b_gate[j], ml_conv[j],
                                       ml_g_head[j], ml_w_o[j], **dims)
        h_lat = ffn_residual(h_lat, g_pre2, g_post2, mod, ff_w1, ff_w2, i, **lat)
        if not last:
            h_ctx = ffn_residual(h_ctx, g_pre2, g_post2, mod, ff_w1, ff_w2, i, **cst)
    return h_lat.reshape(batch, seq, d)
```

```python
import functools
import math
from typing import Any, NamedTuple

import numpy as np
import jax
import jax.numpy as jnp
from jax import lax
from jax.experimental import pallas as pl
from jax.experimental.pallas import tpu as pltpu

F32 = jnp.float32
BF16 = jnp.bfloat16

EPS = 1e-6
N_MOD = 6
GRID_W = 64

NA_HEADS = 16
NA_WIN_R = 8
NA_WIN_C = 16
NA_Q_ROWS = 4
NA_K_ROWS = 12

MLA_HEADS = 16
MLA_Q_RANK = 512
MLA_KV_RANK = 512
MLA_NOPE = 128
MLA_ROPE = 64
MLA_V = 128
ROPE_PAIRS = MLA_ROPE // 4
ROPE_BASE = 10000.0
MLA_QK_PAD = 256

ML_HEADS = 8
ML_QK = 128
ML_V = 256
ML_CONV = 3
ML_CHUNK = 256

LANE = 128
NEG_BIG = -1e30
VMEM_LIMIT_BYTES = 56 * 1024 * 1024


def _cparams(*sem):
    return pltpu.CompilerParams(dimension_semantics=sem, vmem_limit_bytes=VMEM_LIMIT_BYTES)


def _rms(x, g):
    return x * lax.rsqrt(jnp.mean(x * x, axis=-1, keepdims=True) + EPS) * g


def _dot(a, b):
    return jnp.dot(a, b, preferred_element_type=F32)


def _dot_nt(a, b):
    return lax.dot_general(a, b, (((1,), (1,)), ((), ())), preferred_element_type=F32)


def _dot_tn(a, b):
    return lax.dot_general(a, b, (((0,), (0,)), ((), ())), preferred_element_type=F32)


def _mod_index_map(tm, rows_per_mod, mod_base):
    return lambda i, *_: (mod_base + (i * tm) // rows_per_mod, 0, 0)


def _ada_kernel(c_ref, w_ref, b_ref, o_ref):
    c = c_ref[...]
    s = c * jax.nn.sigmoid(c)
    o_ref[...] = _dot(s, w_ref[...]) + b_ref[...]


def ada_modulation(c_all, ada_w, ada_b, *, tn=1024):
    depth, d, n = ada_w.shape
    r = c_all.shape[0]
    return pl.pallas_call(
        _ada_kernel,
        grid=(depth, n // tn),
        in_specs=[
            pl.BlockSpec((r, d), lambda l, j: (0, 0)),
            pl.BlockSpec((None, d, tn), lambda l, j: (l, 0, j)),
            pl.BlockSpec((None, 1, tn), lambda l, j: (l, 0, j)),
        ],
        out_specs=pl.BlockSpec((None, r, tn), lambda l, j: (l, 0, j)),
        out_shape=jax.ShapeDtypeStruct((depth, r, n), F32),
        compiler_params=_cparams("arbitrary", "arbitrary"),
        name="ada_modulation",
    )(c_all, ada_w, ada_b.reshape(depth, 1, n))


class ProjOut(NamedTuple):
    w: jax.Array
    dtype: Any
    transposed: bool = False
    scale: float = 1.0


def _norm_proj_kernel(*refs, outs, narrow, shift_idx, scale_idx):
    n_w = len(outs) + (narrow is not None)
    n_in = 2 + (shift_idx is not None)
    x_ref, g_ref = refs[:2]
    mod_ref = refs[2] if shift_idx is not None else None
    w_refs = refs[n_in:n_in + n_w]
    o_refs = refs[n_in + n_w:n_in + 2 * n_w]
    a_scr = refs[-1]

    def emit(o_ref, w_ref, spec):
        y = _dot(a_scr[...], w_ref[...])
        if spec.scale != 1.0:
            y = y * spec.scale
        o_ref[...] = (y.T if spec.transposed else y).astype(o_ref.dtype)

    @pl.when(pl.program_id(1) == 0)
    def _():
        y = _rms(x_ref[...], g_ref[...])
        if shift_idx is not None:
            y = y * (1.0 + mod_ref[scale_idx:scale_idx + 1, :]) + mod_ref[shift_idx:shift_idx + 1, :]
        a_scr[...] = y.astype(BF16)
        if narrow is not None:
            emit(o_refs[-1], w_refs[-1], narrow)

    for o_ref, w_ref, spec in zip(o_refs, w_refs, outs):
        emit(o_ref, w_ref, spec)


def norm_proj(x, g, outs, *, tm, tn, narrow=None, mod=None, rows_per_mod=None, mod_base=0,
              shift_idx=None, scale_idx=None, x_col_block=0):
    rows = x.shape[0]
    k, n = outs[0].w.shape
    assert all(o.w.shape == (k, n) for o in outs)
    tm = min(tm, rows)
    tn = min(tn, n)
    in_specs = [
        pl.BlockSpec((tm, k), lambda i, j: (i, x_col_block)),
        pl.BlockSpec((1, k), lambda i, j: (0, 0)),
    ]
    args = [x, g.reshape(1, k)]
    if shift_idx is not None:
        in_specs.append(pl.BlockSpec((None, N_MOD, k), _mod_index_map(tm, rows_per_mod, mod_base)))
        args.append(mod)
    out_specs, out_shapes = [], []
    for o in outs:
        in_specs.append(pl.BlockSpec((k, tn), lambda i, j: (0, j)))
        args.append(o.w)
        if o.transposed:
            out_specs.append(pl.BlockSpec((tn, tm), lambda i, j: (j, i)))
            out_shapes.append(jax.ShapeDtypeStruct((n, rows), o.dtype))
        else:
            out_specs.append(pl.BlockSpec((tm, tn), lambda i, j: (i, j)))
            out_shapes.append(jax.ShapeDtypeStruct((rows, n), o.dtype))
    if narrow is not None:
        assert not narrow.transposed
        nn = narrow.w.shape[1]
        in_specs.append(pl.BlockSpec((k, nn), lambda i, j: (0, 0)))
        args.append(narrow.w)
        out_specs.append(pl.BlockSpec((tm, nn), lambda i, j: (i, 0)))
        out_shapes.append(jax.ShapeDtypeStruct((rows, nn), narrow.dtype))
    strip = lambda o: o._replace(w=None)
    return pl.pallas_call(
        functools.partial(_norm_proj_kernel, outs=tuple(strip(o) for o in outs),
                          narrow=None if narrow is None else strip(narrow),
                          shift_idx=shift_idx, scale_idx=scale_idx),
        grid=(rows // tm, n // tn),
        in_specs=in_specs,
        out_specs=out_specs,
        out_shape=out_shapes,
        scratch_shapes=[pltpu.VMEM((tm, k), BF16)],
        compiler_params=_cparams("parallel", "arbitrary"),
        name="norm_proj",
    )(*args)


def _out_proj_kernel(o_ref, w_ref, h_ref, g_ref, mod_ref, out_ref, *, gate_idx, o_transposed):
    y = _dot_tn(o_ref[...], w_ref[...]) if o_transposed else _dot(o_ref[...], w_ref[...])
    out_ref[...] = h_ref[...] + mod_ref[gate_idx:gate_idx + 1, :] * _rms(y, g_ref[...])


def out_proj_residual(o, w, h, g, mod, *, gate_idx, rows_per_mod, mod_base, o_transposed=False, tm=512):
    k, d = w.shape
    rows = h.shape[0]
    tm = min(tm, rows)
    o_spec = pl.BlockSpec((k, tm), lambda i: (0, i)) if o_transposed else pl.BlockSpec((tm, k), lambda i: (i, 0))
    return pl.pallas_call(
        functools.partial(_out_proj_kernel, gate_idx=gate_idx, o_transposed=o_transposed),
        grid=(rows // tm,),
        in_specs=[
            o_spec,
            pl.BlockSpec((k, d), lambda i: (0, 0)),
            pl.BlockSpec((tm, d), lambda i: (i, 0)),
            pl.BlockSpec((1, d), lambda i: (0, 0)),
            pl.BlockSpec((None, N_MOD, d), _mod_index_map(tm, rows_per_mod, mod_base)),
        ],
        out_specs=pl.BlockSpec((tm, d), lambda i: (i, 0)),
        out_shape=jax.ShapeDtypeStruct((rows, d), F32),
        compiler_params=_cparams("parallel"),
        name="out_proj_residual",
    )(o, w, h, g.reshape(1, d), mod)


def _ffn_kernel(x_ref, g1_ref, mod_ref, w1_ref, w2_ref, g2_ref, out_ref, a_scr,
                *, shift_idx, scale_idx, gate_idx):
    f = pl.program_id(1)
    last = pl.num_programs(1) - 1
    tm = x_ref.shape[0]
    n_slices = 2 if tm % 512 == 0 else 1
    ts = tm // n_slices

    def make_mlp():
        w1 = w1_ref[...].astype(BF16)
        w2 = w2_ref[...].astype(BF16)

        def mlp(a):
            hid = jnp.maximum(_dot(a, w1), 0.0)
            return _dot((hid * hid).astype(BF16), w2)
        return mlp

    @pl.when(f == 0)
    def _():
        mlp = make_mlp()
        for r in range(n_slices):
            rows = slice(r * ts, (r + 1) * ts)
            y = _rms(x_ref[rows, :], g1_ref[...])
            y = y * (1.0 + mod_ref[scale_idx:scale_idx + 1, :]) + mod_ref[shift_idx:shift_idx + 1, :]
            a = y.astype(BF16)
            a_scr[rows, :] = a
            out_ref[rows, :] = mlp(a)

    @pl.when(jnp.logical_and(f != 0, f != last))
    def _():
        out_ref[...] += make_mlp()(a_scr[...])

    @pl.when(jnp.logical_and(f != 0, f == last))
    def _():
        mlp = make_mlp()
        for r in range(n_slices):
            rows = slice(r * ts, (r + 1) * ts)
            y = out_ref[rows, :] + mlp(a_scr[rows, :])
            out_ref[rows, :] = x_ref[rows, :] + mod_ref[gate_idx:gate_idx + 1, :] * _rms(y, g2_ref[...])


def ffn_residual(x, g_pre, g_post, mod, w1_all, w2_all, layer, *, rows_per_mod, mod_base, tm=1024, tf=512):
    rows, d = x.shape
    dff = w1_all.shape[2]
    tm = min(tm, rows)
    assert dff // tf >= 2
    return pl.pallas_call(
        functools.partial(_ffn_kernel, shift_idx=3, scale_idx=4, gate_idx=5),
        grid=(rows // tm, dff // tf),
        in_specs=[
            pl.BlockSpec((tm, d), lambda i, f: (i, 0), pipeline_mode=pl.Buffered(1)),
            pl.BlockSpec((1, d), lambda i, f: (0, 0)),
            pl.BlockSpec((None, N_MOD, d), _mod_index_map(tm, rows_per_mod, mod_base)),
            pl.BlockSpec((None, d, tf), lambda i, f: (layer, 0, f)),
            pl.BlockSpec((None, tf, d), lambda i, f: (layer, f, 0)),
            pl.BlockSpec((1, d), lambda i, f: (0, 0)),
        ],
        out_specs=pl.BlockSpec((tm, d), lambda i, f: (i, 0)),
        out_shape=jax.ShapeDtypeStruct((rows, d), F32),
        scratch_shapes=[pltpu.VMEM((tm, d), BF16)],
        compiler_params=_cparams("parallel", "arbitrary"),
        name="ffn_residual",
    )(x, g_pre.reshape(1, d), mod, w1_all, w2_all, g_post.reshape(1, d))


def _ctx_attn_kernel(q_ref, k_ref, v_ref, o_ref, *, scale, base2):
    s = _dot_nt(q_ref[...], k_ref[...])
    if scale != 1.0:
        s = s * scale
    m = jnp.max(s, axis=-1, keepdims=True)
    p = jnp.exp2(s - m) if base2 else jnp.exp(s - m)
    l = jnp.sum(p, axis=-1, keepdims=True)
    o_ref[...] = (_dot(p.astype(BF16), v_ref[...]) / l).astype(o_ref.dtype)


def ctx_attention(q_arr, k_arr, v_arr, *, batch, heads, lc, dq, dv, q_off, k_off, v_off, scale, base2=False):
    return pl.pallas_call(
        functools.partial(_ctx_attn_kernel, scale=scale, base2=base2),
        grid=(batch, heads),
        in_specs=[
            pl.BlockSpec((lc, dq), lambda b, h: (b, q_off + h)),
            pl.BlockSpec((lc, dq), lambda b, h: (b, k_off + h)),
            pl.BlockSpec((lc, dv), lambda b, h: (b, v_off + h)),
        ],
        out_specs=pl.BlockSpec((lc, dv), lambda b, h: (b, h)),
        out_shape=jax.ShapeDtypeStruct((batch * lc, heads * dv), BF16),
        compiler_params=_cparams("parallel", "parallel"),
        name="ctx_attention",
    )(q_arr, k_arr, v_arr)


def _na_window_tables(img_rows):
    kr = min(NA_WIN_R, img_rows)
    n_blk = img_rows // NA_Q_ROWS

    def tables(i):
        ks = int(np.clip(i * NA_Q_ROWS - NA_WIN_R // 2, 0, img_rows - NA_K_ROWS))
        r = i * NA_Q_ROWS + np.arange(NA_Q_ROWS)[:, None]
        kr_abs = ks + np.arange(NA_K_ROWS)[None, :]
        rs = np.clip(r - kr // 2, 0, img_rows - kr)
        valid = (kr_abs >= rs) & (kr_abs < rs + kr)
        dr = np.clip(kr_abs - r + NA_WIN_R - 1, 0, 2 * NA_WIN_R - 2)
        return ks, valid, dr

    first, inner, last = tables(0), tables(1), tables(n_blk - 1)
    for i in range(1, n_blk - 1):
        ks, valid, dr = tables(i)
        assert ks == i * NA_Q_ROWS - NA_WIN_R // 2
        assert (valid == inner[1]).all() and (dr[valid] == inner[2][valid]).all()
    assert first[0] == 0 and last[0] == img_rows - NA_K_ROWS
    return [first, inner, last]


def _na_bias_tables(rpb, img_rows):
    qcol = np.arange(GRID_W)
    col_start = np.clip(qcol - NA_WIN_C // 2, 0, GRID_W - NA_WIN_C)
    col_mask = (qcol[None, :] >= col_start[:, None]) & (qcol[None, :] < col_start[:, None] + NA_WIN_C)
    dc_idx = np.clip(qcol[None, :] - qcol[:, None], -(NA_WIN_C - 1), NA_WIN_C - 1) + NA_WIN_C - 1
    rpb_cols = rpb[:, :, dc_idx]
    out = []
    for _, valid, dr in _na_window_tables(img_rows):
        t = rpb_cols[:, dr]
        mask = valid[:, :, None, None] & col_mask[None, None]
        t = jnp.where(mask[None], t * math.log2(math.e), NEG_BIG)
        t = t.transpose(0, 2, 4, 1, 3).reshape(rpb.shape[0], NA_K_ROWS * GRID_W, NA_Q_ROWS * GRID_W)
        out.append(t)
    return jnp.stack(out, axis=1)


def _na_kernel(q_ref, k_ref, vt_ref, kc_ref, vct_ref, bias_ref, ot_ref, *, img_rows):
    qb = NA_Q_ROWS * GRID_W
    kb = NA_K_ROWS * GRID_W
    n_blk = img_rows // NA_Q_ROWS
    kc = kc_ref[...]
    vct = vct_ref[...]

    def offsets(i):
        if isinstance(i, int):
            return i * qb, min(max(i * NA_Q_ROWS - NA_WIN_R // 2, 0), img_rows - NA_K_ROWS) * GRID_W
        q0 = pl.multiple_of(i * qb, qb)
        return q0, pl.multiple_of(q0 - (NA_WIN_R // 2) * GRID_W, qb)

    def logits(i, variant):
        q0, ks = offsets(i)
        q = q_ref[pl.ds(q0, qb), :]
        return _dot_nt(k_ref[pl.ds(ks, kb), :], q) + bias_ref[variant], _dot_nt(kc, q)

    def softmax(s_w, s_c):
        m = jnp.maximum(jnp.max(s_w, axis=0, keepdims=True), jnp.max(s_c, axis=0, keepdims=True))
        p_w = jnp.exp2(s_w - m)
        p_c = jnp.exp2(s_c - m)
        l = jnp.sum(p_w, axis=0, keepdims=True) + jnp.sum(p_c, axis=0, keepdims=True)
        return p_w.astype(BF16), p_c.astype(BF16), l

    def weighted_values(i, p_w, p_c, l):
        q0, ks = offsets(i)
        o = _dot(vt_ref[:, pl.ds(ks, kb)], p_w) + _dot(vct, p_c)
        ot_ref[:, pl.ds(q0, qb)] = (o / l).astype(ot_ref.dtype)

    def group(blocks):
        n = len(blocks)
        s, p = {}, {}
        for step in range(n + 2):
            if step < n:
                s[step] = logits(*blocks[step])
            if 1 <= step <= n:
                p[step - 1] = softmax(*s.pop(step - 1))
            if 2 <= step:
                weighted_values(blocks[step - 2][0], *p.pop(step - 2))

    group([(0, 0), (n_blk - 1, 2)])
    n_inner = n_blk - 2
    per_iter = next(g for g in (5, 4, 3, 2, 1) if n_inner % g == 0)

    def body(it, carry):
        group([(1 + it * per_iter + j, 1) for j in range(per_iter)])
        return carry

    lax.fori_loop(0, n_inner // per_iter, body, 0)


def na_attention(q_lat, k_lat, vt_lat, k_ctx, vt_ctx, bias, *, batch, seq, lc):
    heads = NA_HEADS
    dh = q_lat.shape[1] // heads
    img_rows = seq // GRID_W
    qb = NA_Q_ROWS * GRID_W
    kb = NA_K_ROWS * GRID_W
    assert (NA_WIN_R // 2) * GRID_W == qb
    return pl.pallas_call(
        functools.partial(_na_kernel, img_rows=img_rows),
        grid=(batch, heads),
        in_specs=[
            pl.BlockSpec((seq, dh), lambda b, h: (b, h)),
            pl.BlockSpec((seq, dh), lambda b, h: (b, h)),
            pl.BlockSpec((dh, seq), lambda b, h: (h, b)),
            pl.BlockSpec((lc, dh), lambda b, h: (b, h)),
            pl.BlockSpec((dh, lc), lambda b, h: (h, b)),
            pl.BlockSpec((None, 3, kb, qb), lambda b, h: (h, 0, 0, 0)),
        ],
        out_specs=pl.BlockSpec((dh, seq), lambda b, h: (h, b)),
        out_shape=jax.ShapeDtypeStruct((heads * dh, batch * seq), BF16),
        compiler_params=_cparams("parallel", "parallel"),
        name="na_attention",
    )(q_lat, k_lat, vt_lat, k_ctx, vt_ctx, bias)


def _mla_up_kernel(z_ref, gq_ref, gkv_ref, wq_ref, wkv_ref, cos_ref, sin_ref,
                   q_ref, k_ref, vt_ref, *maybe_v_ref, scale):
    hq = MLA_HEADS * MLA_QK_PAD
    z = z_ref[...]
    cq = _rms(z[:, :MLA_Q_RANK], gq_ref[...]).astype(BF16)
    ckv = _rms(z[:, MLA_Q_RANK:MLA_Q_RANK + MLA_KV_RANK], gkv_ref[...]).astype(BF16)
    cos = cos_ref[...]
    sin = sin_ref[...]
    c0 = MLA_Q_RANK + MLA_KV_RANK
    kpe = (z[:, c0:c0 + LANE] * cos + z[:, c0 + LANE:c0 + 2 * LANE] * sin).astype(BF16)
    qraw = _dot(cq, wq_ref[...])
    kv = _dot(ckv, wkv_ref[...])
    for h in range(MLA_HEADS):
        a = h * MLA_QK_PAD
        q_ref[:, a:a + LANE] = (qraw[:, a:a + LANE] * scale).astype(BF16)
        pe = qraw[:, a + LANE:a + 2 * LANE]
        sw = qraw[:, hq + h * LANE:hq + (h + 1) * LANE]
        q_ref[:, a + LANE:a + 2 * LANE] = ((pe * cos + sw * sin) * scale).astype(BF16)
        k_ref[:, a:a + LANE] = kv[:, h * MLA_NOPE:(h + 1) * MLA_NOPE].astype(BF16)
        k_ref[:, a + LANE:a + 2 * LANE] = kpe
    v = kv[:, MLA_HEADS * MLA_NOPE:]
    vt_ref[...] = v.T.astype(BF16)
    for v_ref in maybe_v_ref:
        v_ref[...] = v.astype(BF16)


def mla_up(z, g_q, g_kv, wq, wkv, cos, sin, *, with_v, tm=256):
    rows = z.shape[0]
    tm = min(tm, rows)
    n_pos = cos.shape[0] // tm
    hq = MLA_HEADS * MLA_QK_PAD
    hv = MLA_HEADS * MLA_V
    v_specs = [pl.BlockSpec((tm, hv), lambda i: (i, 0))] if with_v else []
    v_shapes = [jax.ShapeDtypeStruct((rows, hv), BF16)] if with_v else []
    return pl.pallas_call(
        functools.partial(_mla_up_kernel, scale=(MLA_NOPE + MLA_ROPE) ** -0.5 * math.log2(math.e)),
        grid=(rows // tm,),
        in_specs=[
            pl.BlockSpec((tm, z.shape[1]), lambda i: (i, 0)),
            pl.BlockSpec((1, MLA_Q_RANK), lambda i: (0, 0)),
            pl.BlockSpec((1, MLA_KV_RANK), lambda i: (0, 0)),
            pl.BlockSpec(wq.shape, lambda i: (0, 0)),
            pl.BlockSpec(wkv.shape, lambda i: (0, 0)),
            pl.BlockSpec((tm, LANE), lambda i: (i % n_pos, 0)),
            pl.BlockSpec((tm, LANE), lambda i: (i % n_pos, 0)),
        ],
        out_specs=[
            pl.BlockSpec((tm, hq), lambda i: (i, 0)),
            pl.BlockSpec((tm, hq), lambda i: (i, 0)),
            pl.BlockSpec((hv, tm), lambda i: (0, i)),
        ] + v_specs,
        out_shape=[
            jax.ShapeDtypeStruct((rows, hq), BF16),
            jax.ShapeDtypeStruct((rows, hq), BF16),
            jax.ShapeDtypeStruct((hv, rows), BF16),
        ] + v_shapes,
        compiler_params=_cparams("parallel"),
        name="mla_up",
    )(z, g_q.reshape(1, -1), g_kv.reshape(1, -1), wq, wkv, cos, sin)


def _mla_attn_kernel(q_ref, k_ref, vt_ref, kc_ref, vct_ref, ot_ref, *, tk):
    q = q_ref[...]
    n_tiles = 1 + k_ref.shape[0] // tk

    def logits(j):
        k = kc_ref[...] if j == 0 else k_ref[(j - 1) * tk:j * tk, :]
        return _dot_nt(k, q)

    def values(j):
        return vct_ref[...] if j == 0 else vt_ref[:, (j - 1) * tk:j * tk]

    m = l = acc = pending = None
    st = {0: logits(0)}
    for step in range(n_tiles + 1):
        if step + 1 < n_tiles:
            st[step + 1] = logits(step + 1)
        if step < n_tiles:
            s = st.pop(step)
            m_tile = jnp.max(s, axis=0, keepdims=True)
            m_new = m_tile if m is None else jnp.maximum(m, m_tile)
            alpha = None if m is None else jnp.exp2(m - m_new)
            p = jnp.exp2(s - m_new)
            m = m_new
            done_softmax = (step, alpha, p.astype(BF16), jnp.sum(p, axis=0, keepdims=True))
        if pending is not None:
            j, alpha_j, p_j, p_sum = pending
            pv = _dot(values(j), p_j)
            l, acc = (p_sum, pv) if acc is None else (alpha_j * l + p_sum, alpha_j * acc + pv)
        pending = done_softmax if step < n_tiles else None
    ot_ref[...] = (acc / l).astype(ot_ref.dtype)


def mla_attention(q_lat, k_lat, vt_lat, k_ctx, vt_ctx, *, batch, seq, lc, tq=512, tk=1024):
    heads = MLA_HEADS
    tq = min(tq, seq)
    tk = min(tk, seq)
    nq = seq // tq
    return pl.pallas_call(
        functools.partial(_mla_attn_kernel, tk=tk),
        grid=(batch, heads, nq),
        in_specs=[
            pl.BlockSpec((tq, MLA_QK_PAD), lambda b, h, i: (b * nq + i, h)),
            pl.BlockSpec((seq, MLA_QK_PAD), lambda b, h, i: (b, h)),
            pl.BlockSpec((MLA_V, seq), lambda b, h, i: (h, b)),
            pl.BlockSpec((lc, MLA_QK_PAD), lambda b, h, i: (b, h)),
            pl.BlockSpec((MLA_V, lc), lambda b, h, i: (h, b)),
        ],
        out_specs=pl.BlockSpec((MLA_V, tq), lambda b, h, i: (h, b * nq + i)),
        out_shape=jax.ShapeDtypeStruct((heads * MLA_V, batch * seq), BF16),
        compiler_params=_cparams("parallel", "parallel", "arbitrary"),
        name="mla_attention",
    )(q_lat, k_lat, vt_lat, k_ctx, vt_ctx)


def _rope_tables(seq):
    t = np.arange(seq)
    pos = np.stack([t // GRID_W, t % GRID_W], axis=-1).astype(np.float32)
    freqs = jnp.asarray(ROPE_BASE, F32) ** (-jnp.arange(ROPE_PAIRS, dtype=F32) / ROPE_PAIRS)
    ang = jnp.asarray(pos)[:, :, None] * freqs
    cos, sin = jnp.cos(ang), jnp.sin(ang)
    cos_t = jnp.stack([cos, cos], axis=2).reshape(seq, MLA_ROPE)
    sin_t = jnp.stack([-sin, sin], axis=2).reshape(seq, MLA_ROPE)
    pad = jnp.zeros((seq, LANE - MLA_ROPE), F32)
    return jnp.concatenate([cos_t, pad], axis=1), jnp.concatenate([sin_t, pad], axis=1)


_ROPE_SWAP = np.arange(MLA_ROPE).reshape(2, 2, ROPE_PAIRS)[:, ::-1, :].reshape(-1)


def _mla_weights(w_in, w_uq, w_ukv):
    d = w_in.shape[0]
    c0 = MLA_Q_RANK + MLA_KV_RANK
    zpad = jnp.zeros((d, LANE - MLA_ROPE), w_in.dtype)
    kpe = w_in[:, c0:]
    w_in_x = jnp.concatenate([w_in[:, :c0], kpe, zpad, kpe[:, _ROPE_SWAP], zpad], axis=1)
    r = w_uq.shape[0]
    wq = w_uq.reshape(r, MLA_HEADS, MLA_NOPE + MLA_ROPE)
    zq = jnp.zeros((r, MLA_HEADS, LANE - MLA_ROPE), w_uq.dtype)
    main = jnp.concatenate([wq, zq], axis=2).reshape(r, MLA_HEADS * MLA_QK_PAD)
    swapped = jnp.concatenate([wq[:, :, MLA_NOPE:][:, :, _ROPE_SWAP], zq], axis=2).reshape(r, MLA_HEADS * LANE)
    wq_x = jnp.concatenate([main, swapped], axis=1)
    wkv = w_ukv.reshape(w_ukv.shape[0], MLA_HEADS, MLA_NOPE + MLA_V)
    wkv_x = jnp.concatenate([wkv[:, :, :MLA_NOPE].reshape(r, -1), wkv[:, :, MLA_NOPE:].reshape(r, -1)], axis=1)
    return w_in_x.astype(BF16), wq_x.astype(BF16), wkv_x.astype(BF16)


def _ml_conv_kernel(prev_ref, x_ref, next_ref, w_ref, q_ref, k_ref, *, seg_len, scale):
    tm = x_ref.shape[0]
    hk = ML_HEADS * ML_QK
    x = x_ref[...]
    row = lax.broadcasted_iota(jnp.int32, (tm, 1), 0)
    pos = (pl.program_id(0) * tm) % seg_len + row
    x_prev = jnp.where(row == 0, prev_ref[7:8, :], pltpu.roll(x, 1, 0))
    x_next = jnp.where(row == tm - 1, next_ref[0:1, :], pltpu.roll(x, tm - 1, 0))
    x_prev = jnp.where(pos == 0, 0.0, x_prev)
    x_next = jnp.where(pos == seg_len - 1, 0.0, x_next)
    y = x_prev * w_ref[0:1, :] + x * w_ref[1:2, :] + x_next * w_ref[2:3, :]
    y = y * jax.nn.sigmoid(y)
    q_ref[...] = (y[:, :hk] * scale).astype(BF16)
    k_ref[...] = y[:, hk:].astype(BF16)


def ml_conv_silu(qk_raw, conv_w, *, seg_len, tm=256):
    rows, c = qk_raw.shape
    tm = min(tm, seg_len)
    hk = ML_HEADS * ML_QK
    nb8 = rows // 8
    t8 = tm // 8
    return pl.pallas_call(
        functools.partial(_ml_conv_kernel, seg_len=seg_len, scale=ML_QK ** -0.5),
        grid=(rows // tm,),
        in_specs=[
            pl.BlockSpec((8, c), lambda i: (jnp.maximum(i * t8 - 1, 0), 0)),
            pl.BlockSpec((tm, c), lambda i: (i, 0)),
            pl.BlockSpec((8, c), lambda i: (jnp.minimum((i + 1) * t8, nb8 - 1), 0)),
            pl.BlockSpec((ML_CONV, c), lambda i: (0, 0)),
        ],
        out_specs=[pl.BlockSpec((tm, hk), lambda i: (i, 0)), pl.BlockSpec((tm, hk), lambda i: (i, 0))],
        out_shape=[jax.ShapeDtypeStruct((rows, hk), BF16), jax.ShapeDtypeStruct((rows, hk), BF16)],
        compiler_params=_cparams("parallel"),
        name="ml_conv_silu",
    )(qk_raw, qk_raw, qk_raw, conv_w)


def _log_sigmoid(x):
    return jnp.minimum(x, 0.0) - jnp.log(1.0 + jnp.exp(-jnp.abs(x)))


def _ml_scan_kernel(q_ref, k_ref, v_ref, g_ref, gt_ref, b_ref, bt_ref, c0_ref, n0_ref, m0_ref,
                    h_ref, c_out, n_out, m_out, c_scr, n_scr, m_scr, *, reverse):
    step = pl.program_id(1)
    tc = q_ref.shape[0]

    @pl.when(step == 0)
    def _():
        c_scr[...] = c0_ref[...]
        n_scr[...] = n0_ref[...]
        m_scr[...] = m0_ref[...]

    t_idx = lax.broadcasted_iota(jnp.int32, (tc, tc), 0)
    s_idx = lax.broadcasted_iota(jnp.int32, (tc, tc), 1)
    seen = (s_idx >= t_idx) if reverse else (s_idx <= t_idx)
    seen_t = (t_idx >= s_idx) if reverse else (t_idx <= s_idx)
    gi = 2 * ML_HEADS if reverse else 0
    gf = gi + ML_HEADS

    g = g_ref[...] + b_ref[...]
    gt = gt_ref[...] + bt_ref[...]

    for hd in range(ML_HEADS):
        qh = q_ref[:, hd * ML_QK:(hd + 1) * ML_QK]
        kh = k_ref[:, hd * ML_QK:(hd + 1) * ML_QK]
        vh = v_ref[:, hd * ML_V:(hd + 1) * ML_V]
        li_c = g[:, gi + hd:gi + hd + 1]
        lf_c = _log_sigmoid(g[:, gf + hd:gf + hd + 1])
        li_r = gt[gi + hd:gi + hd + 1, :]
        lf_r = _log_sigmoid(gt[gf + hd:gf + hd + 1, :])
        b_c = jnp.sum(jnp.where(seen, lf_r, 0.0), axis=1, keepdims=True)
        b_r = jnp.sum(jnp.where(seen_t, lf_c, 0.0), axis=0, keepdims=True)
        b_end = jnp.sum(lf_r, axis=1, keepdims=True)
        m_prev = m_scr[hd:hd + 1, 0:1]
        c_prev = c_scr[hd]
        n_prev = n_scr[hd:hd + 1, :]

        log_w = jnp.where(seen, b_c - b_r + li_r, NEG_BIG)
        log_inter = b_c + m_prev
        m_t = jnp.maximum(log_inter, jnp.max(log_w, axis=1, keepdims=True))
        w_inter = jnp.exp(log_inter - m_t)
        s = _dot_nt(qh, kh) * jnp.exp(log_w - m_t)
        num = w_inter * _dot(qh, c_prev.astype(BF16)) + _dot(s.astype(BF16), vh)
        den = (w_inter * jnp.sum(qh.astype(F32) * n_prev, axis=1, keepdims=True)
               + jnp.sum(s, axis=1, keepdims=True))
        h_ref[:, hd * ML_V:(hd + 1) * ML_V] = num / jnp.maximum(jnp.abs(den), jnp.exp(-m_t))

        log_s = b_end - b_c + li_c
        m_new = jnp.maximum(b_end + m_prev, jnp.max(log_s, axis=0, keepdims=True))
        decay = jnp.exp(b_end + m_prev - m_new)
        kw = kh.astype(F32) * jnp.exp(log_s - m_new)
        c_scr[hd] = decay * c_prev + _dot_tn(kw.astype(BF16), vh)
        n_scr[hd:hd + 1, :] = decay * n_prev + jnp.sum(kw, axis=0, keepdims=True)
        m_scr[hd:hd + 1, :] = jnp.broadcast_to(m_new, (1, LANE))

    @pl.when(step == pl.num_programs(1) - 1)
    def _():
        c_out[...] = c_scr[...]
        n_out[...] = n_scr[...]
        m_out[...] = m_scr[...]


def ml_scan(q, k, v, g, gt, b_row, b_col, state, *, batch, seg_len, reverse):
    tc = min(ML_CHUNK, seg_len)
    nc = seg_len // tc
    hk = ML_HEADS * ML_QK
    hv = ML_HEADS * ML_V
    c0, n0, m0 = state

    def blk(b, s):
        return b * nc + ((nc - 1 - s) if reverse else s)

    state_specs = [
        pl.BlockSpec((None, ML_HEADS, ML_QK, ML_V), lambda b, s: (b, 0, 0, 0)),
        pl.BlockSpec((None, ML_HEADS, ML_QK), lambda b, s: (b, 0, 0)),
        pl.BlockSpec((None, ML_HEADS, LANE), lambda b, s: (b, 0, 0)),
    ]
    return pl.pallas_call(
        functools.partial(_ml_scan_kernel, reverse=reverse),
        grid=(batch, nc),
        in_specs=[
            pl.BlockSpec((tc, hk), lambda b, s: (blk(b, s), 0)),
            pl.BlockSpec((tc, hk), lambda b, s: (blk(b, s), 0)),
            pl.BlockSpec((tc, hv), lambda b, s: (blk(b, s), 0)),
            pl.BlockSpec((tc, LANE), lambda b, s: (blk(b, s), 0)),
            pl.BlockSpec((4 * ML_HEADS, tc), lambda b, s: (0, blk(b, s))),
            pl.BlockSpec((1, LANE), lambda b, s: (0, 0)),
            pl.BlockSpec((4 * ML_HEADS, 1), lambda b, s: (0, 0)),
        ] + state_specs,
        out_specs=[pl.BlockSpec((tc, hv), lambda b, s: (blk(b, s), 0))] + state_specs,
        out_shape=[
            jax.ShapeDtypeStruct((batch * seg_len, hv), F32),
            jax.ShapeDtypeStruct(c0.shape, F32),
            jax.ShapeDtypeStruct(n0.shape, F32),
            jax.ShapeDtypeStruct(m0.shape, F32),
        ],
        scratch_shapes=[
            pltpu.VMEM((ML_HEADS, ML_QK, ML_V), F32),
            pltpu.VMEM((ML_HEADS, ML_QK), F32),
            pltpu.VMEM((ML_HEADS, LANE), F32),
        ],
        compiler_params=_cparams("parallel", "arbitrary"),
        name="ml_scan_bwd" if reverse else "ml_scan_fwd",
    )(q, k, v, g, gt, b_row, b_col, c0, n0, m0)


def _ml_out_kernel(hf_ref, hb_ref, og_ref, gh_ref, w_ref, h_ref, g_ref, mod_ref, out_ref, x_scr, *, gate_idx):
    og = og_ref[...]
    sig = jax.nn.sigmoid(og)
    for hd in range(ML_HEADS):
        sl = slice(hd * ML_V, (hd + 1) * ML_V)
        hh = hf_ref[:, sl] + hb_ref[:, sl]
        x_scr[:, sl] = (sig[:, sl] * _rms(hh, gh_ref[:, sl])).astype(BF16)
    y = _dot(x_scr[...], w_ref[...])
    out_ref[...] = h_ref[...] + mod_ref[gate_idx:gate_idx + 1, :] * _rms(y, g_ref[...])


def ml_out_residual(h_f, h_b, og, g_head, w, h, g, mod, *, gate_idx, rows_per_mod, mod_base, tm=256):
    rows, hv = h_f.shape
    d = w.shape[1]
    tm = min(tm, rows)
    return pl.pallas_call(
        functools.partial(_ml_out_kernel, gate_idx=gate_idx),
        grid=(rows // tm,),
        in_specs=[
            pl.BlockSpec((tm, hv), lambda i: (i, 0)),
            pl.BlockSpec((tm, hv), lambda i: (i, 0)),
            pl.BlockSpec((tm, hv), lambda i: (i, 0)),
            pl.BlockSpec((1, hv), lambda i: (0, 0)),
            pl.BlockSpec((hv, d), lambda i: (0, 0)),
            pl.BlockSpec((tm, d), lambda i: (i, 0)),
            pl.BlockSpec((1, d), lambda i: (0, 0)),
            pl.BlockSpec((None, N_MOD, d), _mod_index_map(tm, rows_per_mod, mod_base)),
        ],
        out_specs=pl.BlockSpec((tm, d), lambda i: (i, 0)),
        out_shape=jax.ShapeDtypeStruct((rows, d), F32),
        scratch_shapes=[pltpu.VMEM((tm, hv), BF16)],
        compiler_params=_cparams("parallel"),
        name="ml_out_residual",
    )(h_f, h_b, og, g_head.reshape(1, hv), w, h, g.reshape(1, d), mod)


def _streams(batch, seq, lc):
    return dict(rows_per_mod=seq, mod_base=0), dict(rows_per_mod=batch * lc, mod_base=batch)


def na_mixer(h_lat, h_ctx, mod, g_pre, g_post, w_qkv, rpb, w_o, *, batch, seq, lc, need_ctx):
    lat, ctx = _streams(batch, seq, lc)
    hd = w_o.shape[0]
    dh = hd // NA_HEADS
    w = w_qkv.astype(BF16)
    wo = w_o.astype(BF16)
    q_out = ProjOut(w[:, :hd], BF16, scale=dh ** -0.5 * math.log2(math.e))
    k_out = ProjOut(w[:, hd:2 * hd], BF16)
    v_out = ProjOut(w[:, 2 * hd:], BF16)
    vt_out = v_out._replace(transposed=True)
    proj = functools.partial(norm_proj, g=g_pre, mod=mod, shift_idx=0, scale_idx=1, tn=512)
    q_lat, k_lat, vt_lat = proj(h_lat, outs=[q_out, k_out, vt_out], tm=1024, **lat)
    ctx_outs = [q_out, k_out, vt_out] + ([v_out] if need_ctx else [])
    q_ctx, k_ctx, vt_ctx, *v_ctx = proj(h_ctx, outs=ctx_outs, tm=512, **ctx)
    bias = _na_bias_tables(rpb, seq // GRID_W)
    ot_lat = na_attention(q_lat, k_lat, vt_lat, k_ctx, vt_ctx, bias, batch=batch, seq=seq, lc=lc)
    h_lat = out_proj_residual(ot_lat, wo, h_lat, g_post, mod, gate_idx=2, o_transposed=True, **lat)
    if need_ctx:
        o_ctx = ctx_attention(q_ctx, k_ctx, v_ctx[0], batch=batch, heads=NA_HEADS, lc=lc, dq=dh, dv=dh,
                              q_off=0, k_off=0, v_off=0, scale=1.0, base2=True)
        h_ctx = out_proj_residual(o_ctx, wo, h_ctx, g_post, mod, gate_idx=2, **ctx)
    return h_lat, h_ctx


def mla_mixer(h_lat, h_ctx, mod, g_pre, g_post, w_in, g_q, g_kv, w_uq, w_ukv, w_o, *, batch, seq, lc, need_ctx):
    lat, ctx = _streams(batch, seq, lc)
    w_in_x, wq_x, wkv_x = _mla_weights(w_in, w_uq, w_ukv)
    wo = w_o.astype(BF16)
    proj = functools.partial(norm_proj, g=g_pre, outs=[ProjOut(w_in_x, F32)], mod=mod, shift_idx=0, scale_idx=1,
                             tn=w_in_x.shape[1])
    z_lat, = proj(h_lat, tm=1024, **lat)
    z_ctx, = proj(h_ctx, tm=512, **ctx)
    cos_l, sin_l = _rope_tables(seq)
    ones = jnp.concatenate([jnp.ones((lc, MLA_ROPE), F32), jnp.zeros((lc, LANE - MLA_ROPE), F32)], axis=1)
    q_lat, k_lat, vt_lat = mla_up(z_lat, g_q, g_kv, wq_x, wkv_x, cos_l, sin_l, with_v=False)
    q_ctx, k_ctx, vt_ctx, v_ctx = mla_up(z_ctx, g_q, g_kv, wq_x, wkv_x, ones, jnp.zeros_like(ones), with_v=True)
    ot_lat = mla_attention(q_lat, k_lat, vt_lat, k_ctx, vt_ctx, batch=batch, seq=seq, lc=lc)
    h_lat = out_proj_residual(ot_lat, wo, h_lat, g_post, mod, gate_idx=2, o_transposed=True, **lat)
    if need_ctx:
        o_ctx = ctx_attention(q_ctx, k_ctx, v_ctx, batch=batch, heads=MLA_HEADS, lc=lc, dq=MLA_QK_PAD, dv=MLA_V,
                              q_off=0, k_off=0, v_off=0, scale=1.0, base2=True)
        h_ctx = out_proj_residual(o_ctx, wo, h_ctx, g_post, mod, gate_idx=2, **ctx)
    return h_lat, h_ctx


def mlstm_mixer(h_lat, h_ctx, mod, g_pre, g_post, w_in, b_gate, conv_w, g_head, w_o, *, batch, seq, lc, need_ctx):
    lat, ctx = _streams(batch, seq, lc)
    d = w_in.shape[0]
    hk2 = 2 * ML_HEADS * ML_QK
    hv = ML_HEADS * ML_V
    ng = 4 * ML_HEADS
    w_qk = w_in[:, :hk2].astype(BF16)
    w_v = w_in[:, hk2:hk2 + hv].astype(BF16)
    w_og = w_in[:, hk2 + hv:hk2 + 2 * hv].astype(BF16)
    w_g = jnp.concatenate([w_in[:, hk2 + 2 * hv:], jnp.zeros((d, LANE - ng), w_in.dtype)], axis=1).astype(BF16)
    wo = w_o.astype(BF16)
    b_row = jnp.concatenate([b_gate, jnp.zeros((LANE - ng,), F32)]).reshape(1, LANE)
    b_col = b_gate.reshape(ng, 1)

    def project(h, stream, seg_len):
        qk_raw, v, og, g = norm_proj(
            h, g_pre, [ProjOut(w_qk, F32), ProjOut(w_v, BF16), ProjOut(w_og, F32)], narrow=ProjOut(w_g, F32),
            mod=mod, shift_idx=0, scale_idx=1, tm=512, tn=512, **stream)
        q, k = ml_conv_silu(qk_raw, conv_w, seg_len=seg_len)
        return q, k, v, og, g, g[:, :ng].T

    def bidir(q, k, v, g, gt, st_f, st_b, seg_len):
        h_f, *st_f = ml_scan(q, k, v, g, gt, b_row, b_col, st_f, batch=batch, seg_len=seg_len, reverse=False)
        h_b, *st_b = ml_scan(q, k, v, g, gt, b_row, b_col, st_b, batch=batch, seg_len=seg_len, reverse=True)
        return h_f, h_b, st_f, st_b

    zero = (jnp.zeros((batch, ML_HEADS, ML_QK, ML_V), F32), jnp.zeros((batch, ML_HEADS, ML_QK), F32),
            jnp.zeros((batch, ML_HEADS, LANE), F32))
    qc, kc, vc, ogc, gc, gtc = project(h_ctx, ctx, lc)
    ql, kl, vl, ogl, gl, gtl = project(h_lat, lat, seq)
    hc_f, hc_b, st_f, st_b = bidir(qc, kc, vc, gc, gtc, zero, zero, lc)
    hl_f, hl_b, _, _ = bidir(ql, kl, vl, gl, gtl, st_f, st_b, seq)
    h_lat = ml_out_residual(hl_f, hl_b, ogl, g_head, wo, h_lat, g_post, mod, gate_idx=2, **lat)
    if need_ctx:
        h_ctx = ml_out_residual(hc_f, hc_b, ogc, g_head, wo, h_ctx, g_post, mod, gate_idx=2, **ctx)
    return h_lat, h_ctx


def kernel(x, c, ctx, c_ctx, ada_w, ada_b, norm_g, ff_w1, ff_w2, na_w_qkv, na_rpb, na_w_o, mla_w_in, mla_g_q, mla_g_kv, mla_w_uq, mla_w_ukv, mla_w_o, ml_w_in, ml_b_gate, ml_conv, ml_g_head, ml_w_o):
    batch, seq, d = x.shape
    lc = ctx.shape[1]
    depth = ada_w.shape[0]
    lat, cst = _streams(batch, seq, lc)

    n_cond = 8 * ((batch + 1 + 7) // 8)
    c_all = jnp.concatenate([c, c_ctx[None, :], jnp.zeros((n_cond - batch - 1, d), F32)], axis=0)
    mods = ada_modulation(c_all, ada_w, ada_b).reshape(depth, n_cond, N_MOD, d)

    h_lat = x.reshape(batch * seq, d)
    h_ctx = ctx.reshape(batch * lc, d)
    for i in range(depth):
        last = i == depth - 1
        mod = mods[i]
        g_pre1, g_post1, g_pre2, g_post2 = norm_g[i]
        kind, j = i % 3, i // 3
        dims = dict(batch=batch, seq=seq, lc=lc, need_ctx=not last)
        if kind == 0:
            h_lat, h_ctx = na_mixer(h_lat, h_ctx, mod, g_pre1, g_post1, na_w_qkv[j], na_rpb[j], na_w_o[j], **dims)
        elif kind == 1:
            h_lat, h_ctx = mla_mixer(h_lat, h_ctx, mod, g_pre1, g_post1, mla_w_in[j], mla_g_q[j], mla_g_kv[j],
                                     mla_w_uq[j], mla_w_ukv[j], mla_w_o[j], **dims)
        else:
            h_lat, h_ctx = mlstm_mixer(h_lat, h_ctx, mod, g_pre1, g_post1, ml_w_in[j], ml_b_gate[j], ml_conv[j],
                                       ml_g_head[j], ml_w_o[j], **dims)
        h_lat = ffn_residual(h_lat, g_pre2, g_post2, mod, ff_w1, ff_w2, i, **lat)
        if not last:
            h_ctx = ffn_residual(h_ctx, g_pre2, g_post2, mod, ff_w1, ff_w2, i, **cst)
    return h_lat.reshape(batch, seq, d)
```

```python
import functools
import math
from typing import Any, NamedTuple

import numpy as np
import jax
import jax.numpy as jnp
from jax import lax
from jax.experimental import pallas as pl
from jax.experimental.pallas import tpu as pltpu

F32 = jnp.float32
BF16 = jnp.bfloat16

EPS = 1e-6
N_MOD = 6
GRID_W = 64

NA_HEADS = 16
NA_WIN_R = 8
NA_WIN_C = 16
NA_Q_ROWS = 4
NA_K_ROWS = 12

MLA_HEADS = 16
MLA_Q_RANK = 512
MLA_KV_RANK = 512
MLA_NOPE = 128
MLA_ROPE = 64
MLA_V = 128
ROPE_PAIRS = MLA_ROPE // 4
ROPE_BASE = 10000.0
MLA_QK_PAD = 256

ML_HEADS = 8
ML_QK = 128
ML_V = 256
ML_CONV = 3
ML_CHUNK = 256

LANE = 128
NEG_BIG = -1e30
VMEM_LIMIT_BYTES = 56 * 1024 * 1024


def _cparams(*sem):
    return pltpu.CompilerParams(dimension_semantics=sem, vmem_limit_bytes=VMEM_LIMIT_BYTES)


def _rms(x, g):
    return x * lax.rsqrt(jnp.mean(x * x, axis=-1, keepdims=True) + EPS) * g


def _dot(a, b):
    return jnp.dot(a, b, preferred_element_type=F32)


def _dot_nt(a, b):
    return lax.dot_general(a, b, (((1,), (1,)), ((), ())), preferred_element_type=F32)


def _dot_tn(a, b):
    return lax.dot_general(a, b, (((0,), (0,)), ((), ())), preferred_element_type=F32)


def _mod_index_map(tm, rows_per_mod, mod_base):
    return lambda i, *_: (mod_base + (i * tm) // rows_per_mod, 0, 0)


def _ada_kernel(c_ref, w_ref, b_ref, o_ref):
    c = c_ref[...]
    s = c * jax.nn.sigmoid(c)
    o_ref[...] = _dot(s, w_ref[...]) + b_ref[...]


def ada_modulation(c_all, ada_w, ada_b, *, tn=1024):
    depth, d, n = ada_w.shape
    r = c_all.shape[0]
    return pl.pallas_call(
        _ada_kernel,
        grid=(depth, n // tn),
        in_specs=[
            pl.BlockSpec((r, d), lambda l, j: (0, 0)),
            pl.BlockSpec((None, d, tn), lambda l, j: (l, 0, j)),
            pl.BlockSpec((None, 1, tn), lambda l, j: (l, 0, j)),
        ],
        out_specs=pl.BlockSpec((None, r, tn), lambda l, j: (l, 0, j)),
        out_shape=jax.ShapeDtypeStruct((depth, r, n), F32),
        compiler_params=_cparams("arbitrary", "arbitrary"),
        name="ada_modulation",
    )(c_all, ada_w, ada_b.reshape(depth, 1, n))


class ProjOut(NamedTuple):
    w: jax.Array
    dtype: Any
    transposed: bool = False
    scale: float = 1.0
    col0: int = 0


def _norm_proj_kernel(*refs, outs, narrow, shift_idx, scale_idx):
    n_w = len(outs) + (narrow is not None)
    n_in = 2 + (shift_idx is not None)
    x_ref, g_ref = refs[:2]
    mod_ref = refs[2] if shift_idx is not None else None
    w_refs = refs[n_in:n_in + n_w]
    o_refs = refs[n_in + n_w:n_in + 2 * n_w]
    a_scr = refs[-1]

    def emit(o_ref, w_ref, spec):
        y = _dot(a_scr[...], w_ref[...])
        if spec.scale != 1.0:
            y = y * spec.scale
        o_ref[...] = (y.T if spec.transposed else y).astype(o_ref.dtype)

    @pl.when(pl.program_id(1) == 0)
    def _():
        y = _rms(x_ref[...], g_ref[...])
        if shift_idx is not None:
            y = y * (1.0 + mod_ref[scale_idx:scale_idx + 1, :]) + mod_ref[shift_idx:shift_idx + 1, :]
        a_scr[...] = y.astype(BF16)
        if narrow is not None:
            emit(o_refs[-1], w_refs[-1], narrow)

    for o_ref, w_ref, spec in zip(o_refs, w_refs, outs):
        emit(o_ref, w_ref, spec)


def norm_proj(x, g, outs, *, tm, tn, n=None, narrow=None, mod=None, rows_per_mod=None, mod_base=0,
              shift_idx=None, scale_idx=None, x_col_block=0):
    rows = x.shape[0]
    k = outs[0].w.shape[0]
    n = outs[0].w.shape[1] if n is None else n
    tm = min(tm, rows)
    tn = min(tn, n)
    assert all(o.w.shape[0] == k and o.col0 % tn == 0 and o.col0 + n <= o.w.shape[1] for o in outs)
    in_specs = [
        pl.BlockSpec((tm, k), lambda i, j: (i, x_col_block)),
        pl.BlockSpec((1, k), lambda i, j: (0, 0)),
    ]
    args = [x, g.reshape(1, k)]
    if shift_idx is not None:
        in_specs.append(pl.BlockSpec((None, N_MOD, k), _mod_index_map(tm, rows_per_mod, mod_base)))
        args.append(mod)
    out_specs, out_shapes = [], []
    for o in outs:
        in_specs.append(pl.BlockSpec((k, tn), lambda i, j, _b0=o.col0 // tn: (0, _b0 + j)))
        args.append(o.w)
        if o.transposed:
            out_specs.append(pl.BlockSpec((tn, tm), lambda i, j: (j, i)))
            out_shapes.append(jax.ShapeDtypeStruct((n, rows), o.dtype))
        else:
            out_specs.append(pl.BlockSpec((tm, tn), lambda i, j: (i, j)))
            out_shapes.append(jax.ShapeDtypeStruct((rows, n), o.dtype))
    if narrow is not None:
        assert not narrow.transposed
        nn = narrow.w.shape[1]
        in_specs.append(pl.BlockSpec((k, nn), lambda i, j: (0, 0)))
        args.append(narrow.w)
        out_specs.append(pl.BlockSpec((tm, nn), lambda i, j: (i, 0)))
        out_shapes.append(jax.ShapeDtypeStruct((rows, nn), narrow.dtype))
    strip = lambda o: o._replace(w=None)
    return pl.pallas_call(
        functools.partial(_norm_proj_kernel, outs=tuple(strip(o) for o in outs),
                          narrow=None if narrow is None else strip(narrow),
                          shift_idx=shift_idx, scale_idx=scale_idx),
        grid=(rows // tm, n // tn),
        in_specs=in_specs,
        out_specs=out_specs,
        out_shape=out_shapes,
        scratch_shapes=[pltpu.VMEM((tm, k), BF16)],
        compiler_params=_cparams("parallel", "arbitrary"),
        name="norm_proj",
    )(*args)


def _out_proj_kernel(o_ref, w_ref, h_ref, g_ref, mod_ref, out_ref, *, gate_idx, o_transposed):
    y = _dot_tn(o_ref[...], w_ref[...]) if o_transposed else _dot(o_ref[...], w_ref[...])
    out_ref[...] = h_ref[...] + mod_ref[gate_idx:gate_idx + 1, :] * _rms(y, g_ref[...])


def out_proj_residual(o, w, h, g, mod, *, gate_idx, rows_per_mod, mod_base, o_transposed=False, tm=512):
    k, d = w.shape
    rows = h.shape[0]
    tm = min(tm, rows)
    o_spec = pl.BlockSpec((k, tm), lambda i: (0, i)) if o_transposed else pl.BlockSpec((tm, k), lambda i: (i, 0))
    return pl.pallas_call(
        functools.partial(_out_proj_kernel, gate_idx=gate_idx, o_transposed=o_transposed),
        grid=(rows // tm,),
        in_specs=[
            o_spec,
            pl.BlockSpec((k, d), lambda i: (0, 0)),
            pl.BlockSpec((tm, d), lambda i: (i, 0)),
            pl.BlockSpec((1, d), lambda i: (0, 0)),
            pl.BlockSpec((None, N_MOD, d), _mod_index_map(tm, rows_per_mod, mod_base)),
        ],
        out_specs=pl.BlockSpec((tm, d), lambda i: (i, 0)),
        out_shape=jax.ShapeDtypeStruct((rows, d), F32),
        compiler_params=_cparams("parallel"),
        name="out_proj_residual",
    )(o, w, h, g.reshape(1, d), mod)


def _ffn_kernel(x_ref, g1_ref, mod_ref, w1_ref, w2_ref, g2_ref, out_ref, a_scr,
                *, shift_idx, scale_idx, gate_idx):
    f = pl.program_id(1)
    last = pl.num_programs(1) - 1
    tm = x_ref.shape[0]
    n_slices = 2 if tm % 512 == 0 else 1
    ts = tm // n_slices

    def make_mlp():
        w1 = w1_ref[...].astype(BF16)
        w2 = w2_ref[...].astype(BF16)

        def mlp(a):
            hid = jnp.maximum(_dot(a, w1), 0.0)
            return _dot((hid * hid).astype(BF16), w2)
        return mlp

    @pl.when(f == 0)
    def _():
        mlp = make_mlp()
        for r in range(n_slices):
            rows = slice(r * ts, (r + 1) * ts)
            y = _rms(x_ref[rows, :], g1_ref[...])
            y = y * (1.0 + mod_ref[scale_idx:scale_idx + 1, :]) + mod_ref[shift_idx:shift_idx + 1, :]
            a = y.astype(BF16)
            a_scr[rows, :] = a
            out_ref[rows, :] = mlp(a)

    @pl.when(jnp.logical_and(f != 0, f != last))
    def _():
        out_ref[...] += make_mlp()(a_scr[...])

    @pl.when(jnp.logical_and(f != 0, f == last))
    def _():
        mlp = make_mlp()
        for r in range(n_slices):
            rows = slice(r * ts, (r + 1) * ts)
            y = out_ref[rows, :] + mlp(a_scr[rows, :])
            out_ref[rows, :] = x_ref[rows, :] + mod_ref[gate_idx:gate_idx + 1, :] * _rms(y, g2_ref[...])


def ffn_residual(x, g_pre, g_post, mod, w1_all, w2_all, layer, *, rows_per_mod, mod_base, tm=1024, tf=512):
    rows, d = x.shape
    dff = w1_all.shape[2]
    tm = min(tm, rows)
    assert dff // tf >= 2
    return pl.pallas_call(
        functools.partial(_ffn_kernel, shift_idx=3, scale_idx=4, gate_idx=5),
        grid=(rows // tm, dff // tf),
        in_specs=[
            pl.BlockSpec((tm, d), lambda i, f: (i, 0), pipeline_mode=pl.Buffered(1)),
            pl.BlockSpec((1, d), lambda i, f: (0, 0)),
            pl.BlockSpec((None, N_MOD, d), _mod_index_map(tm, rows_per_mod, mod_base)),
            pl.BlockSpec((None, d, tf), lambda i, f: (layer, 0, f)),
            pl.BlockSpec((None, tf, d), lambda i, f: (layer, f, 0)),
            pl.BlockSpec((1, d), lambda i, f: (0, 0)),
        ],
        out_specs=pl.BlockSpec((tm, d), lambda i, f: (i, 0)),
        out_shape=jax.ShapeDtypeStruct((rows, d), F32),
        scratch_shapes=[pltpu.VMEM((tm, d), BF16)],
        compiler_params=_cparams("parallel", "arbitrary"),
        name="ffn_residual",
    )(x, g_pre.reshape(1, d), mod, w1_all, w2_all, g_post.reshape(1, d))


def _ctx_attn_kernel(q_ref, k_ref, v_ref, o_ref, *, scale, base2):
    s = _dot_nt(q_ref[...], k_ref[...])
    if scale != 1.0:
        s = s * scale
    m = jnp.max(s, axis=-1, keepdims=True)
    p = jnp.exp2(s - m) if base2 else jnp.exp(s - m)
    l = jnp.sum(p, axis=-1, keepdims=True)
    o_ref[...] = (_dot(p.astype(BF16), v_ref[...]) / l).astype(o_ref.dtype)


def ctx_attention(q_arr, k_arr, v_arr, *, batch, heads, lc, dq, dv, q_off, k_off, v_off, scale, base2=False):
    return pl.pallas_call(
        functools.partial(_ctx_attn_kernel, scale=scale, base2=base2),
        grid=(batch, heads),
        in_specs=[
            pl.BlockSpec((lc, dq), lambda b, h: (b, q_off + h)),
            pl.BlockSpec((lc, dq), lambda b, h: (b, k_off + h)),
            pl.BlockSpec((lc, dv), lambda b, h: (b, v_off + h)),
        ],
        out_specs=pl.BlockSpec((lc, dv), lambda b, h: (b, h)),
        out_shape=jax.ShapeDtypeStruct((batch * lc, heads * dv), BF16),
        compiler_params=_cparams("parallel", "parallel"),
        name="ctx_attention",
    )(q_arr, k_arr, v_arr)


def _na_window_tables(img_rows):
    kr = min(NA_WIN_R, img_rows)
    n_blk = img_rows // NA_Q_ROWS

    def tables(i):
        ks = int(np.clip(i * NA_Q_ROWS - NA_WIN_R // 2, 0, img_rows - NA_K_ROWS))
        r = i * NA_Q_ROWS + np.arange(NA_Q_ROWS)[:, None]
        kr_abs = ks + np.arange(NA_K_ROWS)[None, :]
        rs = np.clip(r - kr // 2, 0, img_rows - kr)
        valid = (kr_abs >= rs) & (kr_abs < rs + kr)
        dr = np.clip(kr_abs - r + NA_WIN_R - 1, 0, 2 * NA_WIN_R - 2)
        return ks, valid, dr

    first, inner, last = tables(0), tables(1), tables(n_blk - 1)
    for i in range(1, n_blk - 1):
        ks, valid, dr = tables(i)
        assert ks == i * NA_Q_ROWS - NA_WIN_R // 2
        assert (valid == inner[1]).all() and (dr[valid] == inner[2][valid]).all()
    assert first[0] == 0 and last[0] == img_rows - NA_K_ROWS
    return [first, inner, last]


def _na_bias_tables(rpb, img_rows):
    qcol = np.arange(GRID_W)
    col_start = np.clip(qcol - NA_WIN_C // 2, 0, GRID_W - NA_WIN_C)
    col_mask = (qcol[None, :] >= col_start[:, None]) & (qcol[None, :] < col_start[:, None] + NA_WIN_C)
    dc_idx = np.clip(qcol[None, :] - qcol[:, None], -(NA_WIN_C - 1), NA_WIN_C - 1) + NA_WIN_C - 1
    rpb_cols = rpb[:, :, dc_idx]
    out = []
    for _, valid, dr in _na_window_tables(img_rows):
        t = rpb_cols[:, dr]
        mask = valid[:, :, None, None] & col_mask[None, None]
        t = jnp.where(mask[None], t * math.log2(math.e), NEG_BIG)
        t = t.transpose(0, 2, 4, 1, 3).reshape(rpb.shape[0], NA_K_ROWS * GRID_W, NA_Q_ROWS * GRID_W)
        out.append(t)
    return jnp.stack(out, axis=1)


def _na_kernel(q_ref, k_ref, vt_ref, kc_ref, vct_ref, bias_ref, ot_ref, *, img_rows):
    qb = NA_Q_ROWS * GRID_W
    kb = NA_K_ROWS * GRID_W
    n_blk = img_rows // NA_Q_ROWS
    kc = kc_ref[...]
    vct = vct_ref[...]

    def offsets(i):
        if isinstance(i, int):
            return i * qb, min(max(i * NA_Q_ROWS - NA_WIN_R // 2, 0), img_rows - NA_K_ROWS) * GRID_W
        q0 = pl.multiple_of(i * qb, qb)
        return q0, pl.multiple_of(q0 - (NA_WIN_R // 2) * GRID_W, qb)

    def logits(i, variant):
        q0, ks = offsets(i)
        q = q_ref[pl.ds(q0, qb), :]
        return _dot_nt(k_ref[pl.ds(ks, kb), :], q) + bias_ref[variant], _dot_nt(kc, q)

    def softmax(s_w, s_c):
        m = jnp.maximum(jnp.max(s_w, axis=0, keepdims=True), jnp.max(s_c, axis=0, keepdims=True))
        p_w = jnp.exp2(s_w - m)
        p_c = jnp.exp2(s_c - m)
        l = jnp.sum(p_w, axis=0, keepdims=True) + jnp.sum(p_c, axis=0, keepdims=True)
        return p_w.astype(BF16), p_c.astype(BF16), l

    def weighted_values(i, p_w, p_c, l):
        q0, ks = offsets(i)
        o = _dot(vt_ref[:, pl.ds(ks, kb)], p_w) + _dot(vct, p_c)
        ot_ref[:, pl.ds(q0, qb)] = (o / l).astype(ot_ref.dtype)

    def group(blocks):
        n = len(blocks)
        s, p = {}, {}
        for step in range(n + 2):
            if step < n:
                s[step] = logits(*blocks[step])
            if 1 <= step <= n:
                p[step - 1] = softmax(*s.pop(step - 1))
            if 2 <= step:
                weighted_values(blocks[step - 2][0], *p.pop(step - 2))

    group([(0, 0), (n_blk - 1, 2)])
    n_inner = n_blk - 2
    per_iter = next(g for g in (5, 4, 3, 2, 1) if n_inner % g == 0)

    def body(it, carry):
        group([(1 + it * per_iter + j, 1) for j in range(per_iter)])
        return carry

    lax.fori_loop(0, n_inner // per_iter, body, 0)


def na_attention(q_lat, k_lat, vt_lat, k_ctx, vt_ctx, bias, *, batch, seq, lc):
    heads = NA_HEADS
    dh = q_lat.shape[1] // heads
    img_rows = seq // GRID_W
    qb = NA_Q_ROWS * GRID_W
    kb = NA_K_ROWS * GRID_W
    assert (NA_WIN_R // 2) * GRID_W == qb
    return pl.pallas_call(
        functools.partial(_na_kernel, img_rows=img_rows),
        grid=(batch, heads),
        in_specs=[
            pl.BlockSpec((seq, dh), lambda b, h: (b, h)),
            pl.BlockSpec((seq, dh), lambda b, h: (b, h)),
            pl.BlockSpec((dh, seq), lambda b, h: (h, b)),
            pl.BlockSpec((lc, dh), lambda b, h: (b, h)),
            pl.BlockSpec((dh, lc), lambda b, h: (h, b)),
            pl.BlockSpec((None, 3, kb, qb), lambda b, h: (h, 0, 0, 0)),
        ],
        out_specs=pl.BlockSpec((dh, seq), lambda b, h: (h, b)),
        out_shape=jax.ShapeDtypeStruct((heads * dh, batch * seq), BF16),
        compiler_params=_cparams("parallel", "parallel"),
        name="na_attention",
    )(q_lat, k_lat, vt_lat, k_ctx, vt_ctx, bias)


def _mla_up_kernel(z_ref, gq_ref, gkv_ref, wq_ref, wkv_ref, cos_ref, sin_ref,
                   q_ref, k_ref, vt_ref, *maybe_v_ref, scale):
    hq = MLA_HEADS * MLA_QK_PAD
    z = z_ref[...]
    cq = _rms(z[:, :MLA_Q_RANK], gq_ref[...]).astype(BF16)
    ckv = _rms(z[:, MLA_Q_RANK:MLA_Q_RANK + MLA_KV_RANK], gkv_ref[...]).astype(BF16)
    cos = cos_ref[...]
    sin = sin_ref[...]
    c0 = MLA_Q_RANK + MLA_KV_RANK
    kpe = (z[:, c0:c0 + LANE] * cos + z[:, c0 + LANE:c0 + 2 * LANE] * sin).astype(BF16)
    qraw = _dot(cq, wq_ref[...])
    kv = _dot(ckv, wkv_ref[...])
    for h in range(MLA_HEADS):
        a = h * MLA_QK_PAD
        q_ref[:, a:a + LANE] = (qraw[:, a:a + LANE] * scale).astype(BF16)
        pe = qraw[:, a + LANE:a + 2 * LANE]
        sw = qraw[:, hq + h * LANE:hq + (h + 1) * LANE]
        q_ref[:, a + LANE:a + 2 * LANE] = ((pe * cos + sw * sin) * scale).astype(BF16)
        k_ref[:, a:a + LANE] = kv[:, h * MLA_NOPE:(h + 1) * MLA_NOPE].astype(BF16)
        k_ref[:, a + LANE:a + 2 * LANE] = kpe
    v = kv[:, MLA_HEADS * MLA_NOPE:]
    vt_ref[...] = v.T.astype(BF16)
    for v_ref in maybe_v_ref:
        v_ref[...] = v.astype(BF16)


def mla_up(z, g_q, g_kv, wq, wkv, cos, sin, *, with_v, tm=256):
    rows = z.shape[0]
    tm = min(tm, rows)
    n_pos = cos.shape[0] // tm
    hq = MLA_HEADS * MLA_QK_PAD
    hv = MLA_HEADS * MLA_V
    v_specs = [pl.BlockSpec((tm, hv), lambda i: (i, 0))] if with_v else []
    v_shapes = [jax.ShapeDtypeStruct((rows, hv), BF16)] if with_v else []
    return pl.pallas_call(
        functools.partial(_mla_up_kernel, scale=(MLA_NOPE + MLA_ROPE) ** -0.5 * math.log2(math.e)),
        grid=(rows // tm,),
        in_specs=[
            pl.BlockSpec((tm, z.shape[1]), lambda i: (i, 0)),
            pl.BlockSpec((1, MLA_Q_RANK), lambda i: (0, 0)),
            pl.BlockSpec((1, MLA_KV_RANK), lambda i: (0, 0)),
            pl.BlockSpec(wq.shape, lambda i: (0, 0)),
            pl.BlockSpec(wkv.shape, lambda i: (0, 0)),
            pl.BlockSpec((tm, LANE), lambda i: (i % n_pos, 0)),
            pl.BlockSpec((tm, LANE), lambda i: (i % n_pos, 0)),
        ],
        out_specs=[
            pl.BlockSpec((tm, hq), lambda i: (i, 0)),
            pl.BlockSpec((tm, hq), lambda i: (i, 0)),
            pl.BlockSpec((hv, tm), lambda i: (0, i)),
        ] + v_specs,
        out_shape=[
            jax.ShapeDtypeStruct((rows, hq), BF16),
            jax.ShapeDtypeStruct((rows, hq), BF16),
            jax.ShapeDtypeStruct((hv, rows), BF16),
        ] + v_shapes,
        compiler_params=_cparams("parallel"),
        name="mla_up",
    )(z, g_q.reshape(1, -1), g_kv.reshape(1, -1), wq, wkv, cos, sin)


def _mla_attn_kernel(q_ref, k_ref, vt_ref, kc_ref, vct_ref, ot_ref, *, tk):
    q = q_ref[...]
    n_tiles = 1 + k_ref.shape[0] // tk

    def logits(j):
        k = kc_ref[...] if j == 0 else k_ref[(j - 1) * tk:j * tk, :]
        return _dot_nt(k, q)

    def values(j):
        return vct_ref[...] if j == 0 else vt_ref[:, (j - 1) * tk:j * tk]

    m = l = acc = pending = None
    st = {0: logits(0)}
    for step in range(n_tiles + 1):
        if step + 1 < n_tiles:
            st[step + 1] = logits(step + 1)
        if step < n_tiles:
            s = st.pop(step)
            m_tile = jnp.max(s, axis=0, keepdims=True)
            m_new = m_tile if m is None else jnp.maximum(m, m_tile)
            alpha = None if m is None else jnp.exp2(m - m_new)
            p = jnp.exp2(s - m_new)
            m = m_new
            done_softmax = (step, alpha, p.astype(BF16), jnp.sum(p, axis=0, keepdims=True))
        if pending is not None:
            j, alpha_j, p_j, p_sum = pending
            pv = _dot(values(j), p_j)
            l, acc = (p_sum, pv) if acc is None else (alpha_j * l + p_sum, alpha_j * acc + pv)
        pending = done_softmax if step < n_tiles else None
    ot_ref[...] = (acc / l).astype(ot_ref.dtype)


def mla_attention(q_lat, k_lat, vt_lat, k_ctx, vt_ctx, *, batch, seq, lc, tq=512, tk=1024):
    heads = MLA_HEADS
    tq = min(tq, seq)
    tk = min(tk, seq)
    nq = seq // tq
    return pl.pallas_call(
        functools.partial(_mla_attn_kernel, tk=tk),
        grid=(batch, heads, nq),
        in_specs=[
            pl.BlockSpec((tq, MLA_QK_PAD), lambda b, h, i: (b * nq + i, h)),
            pl.BlockSpec((seq, MLA_QK_PAD), lambda b, h, i: (b, h)),
            pl.BlockSpec((MLA_V, seq), lambda b, h, i: (h, b)),
            pl.BlockSpec((lc, MLA_QK_PAD), lambda b, h, i: (b, h)),
            pl.BlockSpec((MLA_V, lc), lambda b, h, i: (h, b)),
        ],
        out_specs=pl.BlockSpec((MLA_V, tq), lambda b, h, i: (h, b * nq + i)),
        out_shape=jax.ShapeDtypeStruct((heads * MLA_V, batch * seq), BF16),
        compiler_params=_cparams("parallel", "parallel", "arbitrary"),
        name="mla_attention",
    )(q_lat, k_lat, vt_lat, k_ctx, vt_ctx)


def _rope_tables(seq):
    t = np.arange(seq)
    pos = np.stack([t // GRID_W, t % GRID_W], axis=-1).astype(np.float32)
    freqs = jnp.asarray(ROPE_BASE, F32) ** (-jnp.arange(ROPE_PAIRS, dtype=F32) / ROPE_PAIRS)
    ang = jnp.asarray(pos)[:, :, None] * freqs
    cos, sin = jnp.cos(ang), jnp.sin(ang)
    cos_t = jnp.stack([cos, cos], axis=2).reshape(seq, MLA_ROPE)
    sin_t = jnp.stack([-sin, sin], axis=2).reshape(seq, MLA_ROPE)
    pad = jnp.zeros((seq, LANE - MLA_ROPE), F32)
    return jnp.concatenate([cos_t, pad], axis=1), jnp.concatenate([sin_t, pad], axis=1)


_ROPE_SWAP = np.arange(MLA_ROPE).reshape(2, 2, ROPE_PAIRS)[:, ::-1, :].reshape(-1)


def _mla_weights(w_in, w_uq, w_ukv):
    d = w_in.shape[0]
    c0 = MLA_Q_RANK + MLA_KV_RANK
    zpad = jnp.zeros((d, LANE - MLA_ROPE), w_in.dtype)
    kpe = w_in[:, c0:]
    w_in_x = jnp.concatenate([w_in[:, :c0], kpe, zpad, kpe[:, _ROPE_SWAP], zpad], axis=1)
    r = w_uq.shape[0]
    wq = w_uq.reshape(r, MLA_HEADS, MLA_NOPE + MLA_ROPE)
    zq = jnp.zeros((r, MLA_HEADS, LANE - MLA_ROPE), w_uq.dtype)
    main = jnp.concatenate([wq, zq], axis=2).reshape(r, MLA_HEADS * MLA_QK_PAD)
    swapped = jnp.concatenate([wq[:, :, MLA_NOPE:][:, :, _ROPE_SWAP], zq], axis=2).reshape(r, MLA_HEADS * LANE)
    wq_x = jnp.concatenate([main, swapped], axis=1)
    wkv = w_ukv.reshape(w_ukv.shape[0], MLA_HEADS, MLA_NOPE + MLA_V)
    wkv_x = jnp.concatenate([wkv[:, :, :MLA_NOPE].reshape(r, -1), wkv[:, :, MLA_NOPE:].reshape(r, -1)], axis=1)
    return w_in_x.astype(BF16), wq_x.astype(BF16), wkv_x.astype(BF16)


def _ml_conv_kernel(prev_ref, x_ref, next_ref, w_ref, q_ref, k_ref, *, seg_len, scale):
    tm = x_ref.shape[0]
    hk = ML_HEADS * ML_QK
    x = x_ref[...]
    row = lax.broadcasted_iota(jnp.int32, (tm, 1), 0)
    pos = (pl.program_id(0) * tm) % seg_len + row
    x_prev = jnp.where(row == 0, prev_ref[7:8, :], pltpu.roll(x, 1, 0))
    x_next = jnp.where(row == tm - 1, next_ref[0:1, :], pltpu.roll(x, tm - 1, 0))
    x_prev = jnp.where(pos == 0, 0.0, x_prev)
    x_next = jnp.where(pos == seg_len - 1, 0.0, x_next)
    y = x_prev * w_ref[0:1, :] + x * w_ref[1:2, :] + x_next * w_ref[2:3, :]
    y = y * jax.nn.sigmoid(y)
    q_ref[...] = (y[:, :hk] * scale).astype(BF16)
    k_ref[...] = y[:, hk:].astype(BF16)


def ml_conv_silu(qk_raw, conv_w, *, seg_len, tm=256):
    rows, c = qk_raw.shape
    tm = min(tm, seg_len)
    hk = ML_HEADS * ML_QK
    nb8 = rows // 8
    t8 = tm // 8
    return pl.pallas_call(
        functools.partial(_ml_conv_kernel, seg_len=seg_len, scale=ML_QK ** -0.5),
        grid=(rows // tm,),
        in_specs=[
            pl.BlockSpec((8, c), lambda i: (jnp.maximum(i * t8 - 1, 0), 0)),
            pl.BlockSpec((tm, c), lambda i: (i, 0)),
            pl.BlockSpec((8, c), lambda i: (jnp.minimum((i + 1) * t8, nb8 - 1), 0)),
            pl.BlockSpec((ML_CONV, c), lambda i: (0, 0)),
        ],
        out_specs=[pl.BlockSpec((tm, hk), lambda i: (i, 0)), pl.BlockSpec((tm, hk), lambda i: (i, 0))],
        out_shape=[jax.ShapeDtypeStruct((rows, hk), BF16), jax.ShapeDtypeStruct((rows, hk), BF16)],
        compiler_params=_cparams("parallel"),
        name="ml_conv_silu",
    )(qk_raw, qk_raw, qk_raw, conv_w)


def _log_sigmoid(x):
    return jnp.minimum(x, 0.0) - jnp.log(1.0 + jnp.exp(-jnp.abs(x)))


def _running_sum(x, axis, reverse):
    n = x.shape[axis]
    idx = lax.broadcasted_iota(jnp.int32, x.shape, axis)
    shift = 1
    while shift < n:
        if reverse:
            x = x + jnp.where(idx < n - shift, pltpu.roll(x, n - shift, axis), 0.0)
        else:
            x = x + jnp.where(idx >= shift, pltpu.roll(x, shift, axis), 0.0)
        shift *= 2
    return x


ML_STATE_ROWS = ML_V + 16


def _ml_scan_kernel(q_ref, k_ref, vt_ref, g_ref, gt_ref, b_ref, bt_ref, cn0_ref, m0_ref,
                    ht_ref, cn_out, m_out, cn_scr, m_scr, *, reverse):
    step = pl.program_id(1)
    tc = q_ref.shape[0]

    @pl.when(step == 0)
    def _():
        cn_scr[...] = cn0_ref[...]
        m_scr[...] = m0_ref[...]

    s_idx = lax.broadcasted_iota(jnp.int32, (tc, tc), 0)
    t_idx = lax.broadcasted_iota(jnp.int32, (tc, tc), 1)
    seen = (s_idx >= t_idx) if reverse else (s_idx <= t_idx)
    gi = 2 * ML_HEADS if reverse else 0
    gf = gi + ML_HEADS

    g = g_ref[...] + b_ref[...]
    gt = gt_ref[...] + bt_ref[...]
    lsg_t = _log_sigmoid(gt)
    cum_c = _running_sum(_log_sigmoid(g), 0, reverse)
    cum_r = _running_sum(lsg_t, 1, reverse)

    for hd in range(ML_HEADS):
        qh = q_ref[:, hd * ML_QK:(hd + 1) * ML_QK]
        kh = k_ref[:, hd * ML_QK:(hd + 1) * ML_QK]
        vth = vt_ref[hd * ML_V:(hd + 1) * ML_V, :]
        b_r = cum_r[gf + hd:gf + hd + 1, :]
        b_end = jnp.sum(lsg_t[gf + hd:gf + hd + 1, :], axis=1, keepdims=True)
        col = g[:, gi + hd:gi + hd + 1] - cum_c[:, gf + hd:gf + hd + 1]
        m_prev = m_scr[hd:hd + 1, 0:1]
        cn_prev = cn_scr[hd]

        log_w = jnp.where(seen, b_r + col, NEG_BIG)
        log_inter = b_r + m_prev
        m_t = jnp.maximum(log_inter, jnp.max(log_w, axis=0, keepdims=True))
        w_inter = jnp.exp(log_inter - m_t)
        st = _dot_nt(kh, qh) * jnp.exp(log_w - m_t)
        inter = _dot_nt(cn_prev.astype(BF16), qh)
        num = w_inter * inter[:ML_V, :] + _dot(vth, st.astype(BF16))
        den = w_inter * inter[ML_V:ML_V + 1, :] + jnp.sum(st, axis=0, keepdims=True)
        ht_ref[hd * ML_V:(hd + 1) * ML_V, :] = num / jnp.maximum(jnp.abs(den), jnp.exp(-m_t))

        log_s = col + b_end
        m_new = jnp.maximum(b_end + m_prev, jnp.max(log_s, axis=0, keepdims=True))
        decay = jnp.exp(b_end + m_prev - m_new)
        kw = kh.astype(F32) * jnp.exp(log_s - m_new)
        cn_scr[hd, :ML_V, :] = decay * cn_prev[:ML_V, :] + _dot(vth, kw.astype(BF16))
        cn_scr[hd, ML_V:ML_V + 1, :] = decay * cn_prev[ML_V:ML_V + 1, :] + jnp.sum(kw, axis=0, keepdims=True)
        m_scr[hd:hd + 1, :] = jnp.broadcast_to(m_new, (1, LANE))

    @pl.when(step == pl.num_programs(1) - 1)
    def _():
        cn_out[...] = cn_scr[...]
        m_out[...] = m_scr[...]


def ml_scan(q, k, vt, g, gt, b_row, b_col, state, *, batch, seg_len, reverse):
    tc = min(ML_CHUNK, seg_len)
    nc = seg_len // tc
    hk = ML_HEADS * ML_QK
    hv = ML_HEADS * ML_V
    cn0, m0 = state

    def blk(b, s):
        return b * nc + ((nc - 1 - s) if reverse else s)

    state_specs = [
        pl.BlockSpec((None, ML_HEADS, ML_STATE_ROWS, ML_QK), lambda b, s: (b, 0, 0, 0)),
        pl.BlockSpec((None, ML_HEADS, LANE), lambda b, s: (b, 0, 0)),
    ]
    return pl.pallas_call(
        functools.partial(_ml_scan_kernel, reverse=reverse),
        grid=(batch, nc),
        in_specs=[
            pl.BlockSpec((tc, hk), lambda b, s: (blk(b, s), 0)),
            pl.BlockSpec((tc, hk), lambda b, s: (blk(b, s), 0)),
            pl.BlockSpec((hv, tc), lambda b, s: (0, blk(b, s))),
            pl.BlockSpec((tc, LANE), lambda b, s: (blk(b, s), 0)),
            pl.BlockSpec((4 * ML_HEADS, tc), lambda b, s: (0, blk(b, s))),
            pl.BlockSpec((1, LANE), lambda b, s: (0, 0)),
            pl.BlockSpec((4 * ML_HEADS, 1), lambda b, s: (0, 0)),
        ] + state_specs,
        out_specs=[pl.BlockSpec((hv, tc), lambda b, s: (0, blk(b, s)))] + state_specs,
        out_shape=[
            jax.ShapeDtypeStruct((hv, batch * seg_len), F32),
            jax.ShapeDtypeStruct(cn0.shape, F32),
            jax.ShapeDtypeStruct(m0.shape, F32),
        ],
        scratch_shapes=[
            pltpu.VMEM((ML_HEADS, ML_STATE_ROWS, ML_QK), F32),
            pltpu.VMEM((ML_HEADS, LANE), F32),
        ],
        compiler_params=_cparams("parallel", "arbitrary"),
        name="ml_scan_bwd" if reverse else "ml_scan_fwd",
    )(q, k, vt, g, gt, b_row, b_col, cn0, m0)


def _ml_out_kernel(hf_ref, hb_ref, og_ref, gh_ref, w_ref, h_ref, g_ref, mod_ref, out_ref, x_scr, *, gate_idx):
    for hd in range(ML_HEADS):
        sl = slice(hd * ML_V, (hd + 1) * ML_V)
        hh = hf_ref[sl, :] + hb_ref[sl, :]
        hn = hh * lax.rsqrt(jnp.mean(hh * hh, axis=0, keepdims=True) + EPS) * gh_ref[sl, :]
        x_scr[sl, :] = (jax.nn.sigmoid(og_ref[sl, :]) * hn).astype(BF16)
    y = _dot_tn(x_scr[...], w_ref[...])
    out_ref[...] = h_ref[...] + mod_ref[gate_idx:gate_idx + 1, :] * _rms(y, g_ref[...])


def ml_out_residual(ht_f, ht_b, ogt, g_head, w, h, g, mod, *, gate_idx, rows_per_mod, mod_base, tm=256):
    hv, rows = ht_f.shape
    d = w.shape[1]
    tm = min(tm, rows)
    return pl.pallas_call(
        functools.partial(_ml_out_kernel, gate_idx=gate_idx),
        grid=(rows // tm,),
        in_specs=[
            pl.BlockSpec((hv, tm), lambda i: (0, i)),
            pl.BlockSpec((hv, tm), lambda i: (0, i)),
            pl.BlockSpec((hv, tm), lambda i: (0, i)),
            pl.BlockSpec((hv, 1), lambda i: (0, 0)),
            pl.BlockSpec((hv, d), lambda i: (0, 0)),
            pl.BlockSpec((tm, d), lambda i: (i, 0)),
            pl.BlockSpec((1, d), lambda i: (0, 0)),
            pl.BlockSpec((None, N_MOD, d), _mod_index_map(tm, rows_per_mod, mod_base)),
        ],
        out_specs=pl.BlockSpec((tm, d), lambda i: (i, 0)),
        out_shape=jax.ShapeDtypeStruct((rows, d), F32),
        scratch_shapes=[pltpu.VMEM((hv, tm), BF16)],
        compiler_params=_cparams("parallel"),
        name="ml_out_residual",
    )(ht_f, ht_b, ogt, g_head.reshape(hv, 1), w, h, g.reshape(1, d), mod)


def _streams(batch, seq, lc):
    return dict(rows_per_mod=seq, mod_base=0), dict(rows_per_mod=batch * lc, mod_base=batch)


def na_mixer(h_lat, h_ctx, mod, g_pre, g_post, w_qkv, rpb, w_o, *, batch, seq, lc, need_ctx):
    lat, ctx = _streams(batch, seq, lc)
    hd = w_o.shape[0]
    dh = hd // NA_HEADS
    w = w_qkv.astype(BF16)
    wo = w_o.astype(BF16)
    q_out = ProjOut(w, BF16, scale=dh ** -0.5 * math.log2(math.e))
    k_out = ProjOut(w, BF16, col0=hd)
    v_out = ProjOut(w, BF16, col0=2 * hd)
    vt_out = v_out._replace(transposed=True)
    proj = functools.partial(norm_proj, g=g_pre, mod=mod, shift_idx=0, scale_idx=1, tn=512, n=hd)
    q_lat, k_lat, vt_lat = proj(h_lat, outs=[q_out, k_out, vt_out], tm=1024, **lat)
    ctx_outs = [q_out, k_out, vt_out] + ([v_out] if need_ctx else [])
    q_ctx, k_ctx, vt_ctx, *v_ctx = proj(h_ctx, outs=ctx_outs, tm=512, **ctx)
    bias = _na_bias_tables(rpb, seq // GRID_W)
    ot_lat = na_attention(q_lat, k_lat, vt_lat, k_ctx, vt_ctx, bias, batch=batch, seq=seq, lc=lc)
    h_lat = out_proj_residual(ot_lat, wo, h_lat, g_post, mod, gate_idx=2, o_transposed=True, **lat)
    if need_ctx:
        o_ctx = ctx_attention(q_ctx, k_ctx, v_ctx[0], batch=batch, heads=NA_HEADS, lc=lc, dq=dh, dv=dh,
                              q_off=0, k_off=0, v_off=0, scale=1.0, base2=True)
        h_ctx = out_proj_residual(o_ctx, wo, h_ctx, g_post, mod, gate_idx=2, **ctx)
    return h_lat, h_ctx


def mla_mixer(h_lat, h_ctx, mod, g_pre, g_post, w_in, g_q, g_kv, w_uq, w_ukv, w_o, *, batch, seq, lc, need_ctx):
    lat, ctx = _streams(batch, seq, lc)
    w_in_x, wq_x, wkv_x = _mla_weights(w_in, w_uq, w_ukv)
    wo = w_o.astype(BF16)
    proj = functools.partial(norm_proj, g=g_pre, outs=[ProjOut(w_in_x, F32)], mod=mod, shift_idx=0, scale_idx=1,
                             tn=w_in_x.shape[1])
    z_lat, = proj(h_lat, tm=1024, **lat)
    z_ctx, = proj(h_ctx, tm=512, **ctx)
    cos_l, sin_l = _rope_tables(seq)
    ones = jnp.concatenate([jnp.ones((lc, MLA_ROPE), F32), jnp.zeros((lc, LANE - MLA_ROPE), F32)], axis=1)
    q_lat, k_lat, vt_lat = mla_up(z_lat, g_q, g_kv, wq_x, wkv_x, cos_l, sin_l, with_v=False)
    q_ctx, k_ctx, vt_ctx, v_ctx = mla_up(z_ctx, g_q, g_kv, wq_x, wkv_x, ones, jnp.zeros_like(ones), with_v=True)
    ot_lat = mla_attention(q_lat, k_lat, vt_lat, k_ctx, vt_ctx, batch=batch, seq=seq, lc=lc)
    h_lat = out_proj_residual(ot_lat, wo, h_lat, g_post, mod, gate_idx=2, o_transposed=True, **lat)
    if need_ctx:
        o_ctx = ctx_attention(q_ctx, k_ctx, v_ctx, batch=batch, heads=MLA_HEADS, lc=lc, dq=MLA_QK_PAD, dv=MLA_V,
                              q_off=0, k_off=0, v_off=0, scale=1.0, base2=True)
        h_ctx = out_proj_residual(o_ctx, wo, h_ctx, g_post, mod, gate_idx=2, **ctx)
    return h_lat, h_ctx


def mlstm_mixer(h_lat, h_ctx, mod, g_pre, g_post, w_in, b_gate, conv_w, g_head, w_o, *, batch, seq, lc, need_ctx):
    lat, ctx = _streams(batch, seq, lc)
    d = w_in.shape[0]
    hk2 = 2 * ML_HEADS * ML_QK
    hv = ML_HEADS * ML_V
    ng = 4 * ML_HEADS
    assert hk2 == hv
    w_b = w_in.astype(BF16)
    w_g = jnp.concatenate([w_in[:, hk2 + 2 * hv:], jnp.zeros((d, LANE - ng), w_in.dtype)], axis=1).astype(BF16)
    wo = w_o.astype(BF16)
    b_row = jnp.concatenate([b_gate, jnp.zeros((LANE - ng,), F32)]).reshape(1, LANE)
    b_col = b_gate.reshape(ng, 1)

    def project(h, stream, seg_len):
        outs = [ProjOut(w_b, F32), ProjOut(w_b, BF16, transposed=True, col0=hk2),
                ProjOut(w_b, F32, transposed=True, col0=hk2 + hv)]
        qk_raw, vt, ogt, g = norm_proj(h, g_pre, outs, n=hv, narrow=ProjOut(w_g, F32),
                                       mod=mod, shift_idx=0, scale_idx=1, tm=512, tn=512, **stream)
        q, k = ml_conv_silu(qk_raw, conv_w, seg_len=seg_len)
        return q, k, vt, ogt, g, g[:, :ng].T

    def bidir(q, k, vt, g, gt, st_f, st_b, seg_len):
        h_f, *st_f = ml_scan(q, k, vt, g, gt, b_row, b_col, st_f, batch=batch, seg_len=seg_len, reverse=False)
        h_b, *st_b = ml_scan(q, k, vt, g, gt, b_row, b_col, st_b, batch=batch, seg_len=seg_len, reverse=True)
        return h_f, h_b, st_f, st_b

    zero = (jnp.zeros((batch, ML_HEADS, ML_STATE_ROWS, ML_QK), F32), jnp.zeros((batch, ML_HEADS, LANE), F32))
    qc, kc, vc, ogc, gc, gtc = project(h_ctx, ctx, lc)
    ql, kl, vl, ogl, gl, gtl = project(h_lat, lat, seq)
    hc_f, hc_b, st_f, st_b = bidir(qc, kc, vc, gc, gtc, zero, zero, lc)
    hl_f, hl_b, _, _ = bidir(ql, kl, vl, gl, gtl, st_f, st_b, seq)
    h_lat = ml_out_residual(hl_f, hl_b, ogl, g_head, wo, h_lat, g_post, mod, gate_idx=2, **lat)
    if need_ctx:
        h_ctx = ml_out_residual(hc_f, hc_b, ogc, g_head, wo, h_ctx, g_post, mod, gate_idx=2, **ctx)
    return h_lat, h_ctx


def kernel(x, c, ctx, c_ctx, ada_w, ada_b, norm_g, ff_w1, ff_w2, na_w_qkv, na_rpb, na_w_o, mla_w_in, mla_g_q, mla_g_kv, mla_w_uq, mla_w_ukv, mla_w_o, ml_w_in, ml_b_gate, ml_conv, ml_g_head, ml_w_o):
    batch, seq, d = x.shape
    lc = ctx.shape[1]
    depth = ada_w.shape[0]
    lat, cst = _streams(batch, seq, lc)

    n_cond = 8 * ((batch + 1 + 7) // 8)
    c_all = jnp.concatenate([c, c_ctx[None, :], jnp.zeros((n_cond - batch - 1, d), F32)], axis=0)
    mods = ada_modulation(c_all, ada_w, ada_b).reshape(depth, n_cond, N_MOD, d)

    h_lat = x.reshape(batch * seq, d)
    h_ctx = ctx.reshape(batch * lc, d)
    for i in range(depth):
        last = i == depth - 1
        mod = mods[i]
        g_pre1, g_post1, g_pre2, g_post2 = norm_g[i]
        kind, j = i % 3, i // 3
        dims = dict(batch=batch, seq=seq, lc=lc, need_ctx=not last)
        if kind == 0:
            h_lat, h_ctx = na_mixer(h_lat, h_ctx, mod, g_pre1, g_post1, na_w_qkv[j], na_rpb[j], na_w_o[j], **dims)
        elif kind == 1:
            h_lat, h_ctx = mla_mixer(h_lat, h_ctx, mod, g_pre1, g_post1, mla_w_in[j], mla_g_q[j], mla_g_kv[j],
                                     mla_w_uq[j], mla_w_ukv[j], mla_w_o[j], **dims)
        else:
            h_lat, h_ctx = mlstm_mixer(h_lat, h_ctx, mod, g_pre1, g_post1, ml_w_in[j], ml_b_gate[j], ml_conv[j],
                                       ml_g_head[j], ml_w_o[j], **dims)
        h_lat = ffn_residual(h_lat, g_pre2, g_post2, mod, ff_w1, ff_w2, i, **lat)
        if not last:
            h_ctx = ffn_residual(h_ctx, g_pre2, g_post2, mod, ff_w1, ff_w2, i, **cst)
    return h_lat.reshape(batch, seq, d)
```

```python
import functools
import math
from typing import Any, NamedTuple

import numpy as np
import jax
import jax.numpy as jnp
from jax import lax
from jax.experimental import pallas as pl
from jax.experimental.pallas import tpu as pltpu

F32 = jnp.float32
BF16 = jnp.bfloat16

EPS = 1e-6
N_MOD = 6
GRID_W = 64

NA_HEADS = 16
NA_WIN_R = 8
NA_WIN_C = 16
NA_Q_ROWS = 4
NA_K_ROWS = 12

MLA_HEADS = 16
MLA_Q_RANK = 512
MLA_KV_RANK = 512
MLA_NOPE = 128
MLA_ROPE = 64
MLA_V = 128
ROPE_PAIRS = MLA_ROPE // 4
ROPE_BASE = 10000.0
MLA_QK_PAD = 256

ML_HEADS = 8
ML_QK = 128
ML_V = 256
ML_CONV = 3
ML_CHUNK = 256

LANE = 128
NEG_BIG = -1e30
VMEM_LIMIT_BYTES = 56 * 1024 * 1024


def _cparams(*sem):
    return pltpu.CompilerParams(dimension_semantics=sem, vmem_limit_bytes=VMEM_LIMIT_BYTES)


def _rms(x, g):
    return x * lax.rsqrt(jnp.mean(x * x, axis=-1, keepdims=True) + EPS) * g


def _dot(a, b):
    return jnp.dot(a, b, preferred_element_type=F32)


def _dot_nt(a, b):
    return lax.dot_general(a, b, (((1,), (1,)), ((), ())), preferred_element_type=F32)


def _dot_tn(a, b):
    return lax.dot_general(a, b, (((0,), (0,)), ((), ())), preferred_element_type=F32)


def _mod_index_map(tm, rows_per_mod, mod_base):
    return lambda i, *_: (mod_base + (i * tm) // rows_per_mod, 0, 0)


def _ada_kernel(c_ref, w_ref, b_ref, o_ref):
    c = c_ref[...]
    s = c * jax.nn.sigmoid(c)
    o_ref[...] = _dot(s, w_ref[...]) + b_ref[...]


def ada_modulation(c_all, ada_w, ada_b, *, tn=1024):
    depth, d, n = ada_w.shape
    r = c_all.shape[0]
    return pl.pallas_call(
        _ada_kernel,
        grid=(depth, n // tn),
        in_specs=[
            pl.BlockSpec((r, d), lambda l, j: (0, 0)),
            pl.BlockSpec((None, d, tn), lambda l, j: (l, 0, j)),
            pl.BlockSpec((None, 1, tn), lambda l, j: (l, 0, j)),
        ],
        out_specs=pl.BlockSpec((None, r, tn), lambda l, j: (l, 0, j)),
        out_shape=jax.ShapeDtypeStruct((depth, r, n), F32),
        compiler_params=_cparams("arbitrary", "arbitrary"),
        name="ada_modulation",
    )(c_all, ada_w, ada_b.reshape(depth, 1, n))


class ProjOut(NamedTuple):
    w: jax.Array
    dtype: Any
    transposed: bool = False
    scale: float = 1.0
    col0: int = 0


def _norm_proj_kernel(*refs, outs, narrow, shift_idx, scale_idx):
    n_w = len(outs) + (narrow is not None)
    n_in = 2 + (shift_idx is not None)
    x_ref, g_ref = refs[:2]
    mod_ref = refs[2] if shift_idx is not None else None
    w_refs = refs[n_in:n_in + n_w]
    o_refs = refs[n_in + n_w:n_in + 2 * n_w]
    a_scr = refs[-1]

    def emit(o_ref, w_ref, spec):
        y = _dot(a_scr[...], w_ref[...])
        if spec.scale != 1.0:
            y = y * spec.scale
        o_ref[...] = (y.T if spec.transposed else y).astype(o_ref.dtype)

    @pl.when(pl.program_id(1) == 0)
    def _():
        y = _rms(x_ref[...], g_ref[...])
        if shift_idx is not None:
            y = y * (1.0 + mod_ref[scale_idx:scale_idx + 1, :]) + mod_ref[shift_idx:shift_idx + 1, :]
        a_scr[...] = y.astype(BF16)
        if narrow is not None:
            emit(o_refs[-1], w_refs[-1], narrow)

    for o_ref, w_ref, spec in zip(o_refs, w_refs, outs):
        emit(o_ref, w_ref, spec)


def norm_proj(x, g, outs, *, tm, tn, n=None, narrow=None, mod=None, rows_per_mod=None, mod_base=0,
              shift_idx=None, scale_idx=None, x_col_block=0):
    rows = x.shape[0]
    k = outs[0].w.shape[0]
    n = outs[0].w.shape[1] if n is None else n
    tm = min(tm, rows)
    tn = min(tn, n)
    assert all(o.w.shape[0] == k and o.col0 % tn == 0 and o.col0 + n <= o.w.shape[1] for o in outs)
    in_specs = [
        pl.BlockSpec((tm, k), lambda i, j: (i, x_col_block)),
        pl.BlockSpec((1, k), lambda i, j: (0, 0)),
    ]
    args = [x, g.reshape(1, k)]
    if shift_idx is not None:
        in_specs.append(pl.BlockSpec((None, N_MOD, k), _mod_index_map(tm, rows_per_mod, mod_base)))
        args.append(mod)
    out_specs, out_shapes = [], []
    for o in outs:
        in_specs.append(pl.BlockSpec((k, tn), lambda i, j, _b0=o.col0 // tn: (0, _b0 + j)))
        args.append(o.w)
        if o.transposed:
            out_specs.append(pl.BlockSpec((tn, tm), lambda i, j: (j, i)))
            out_shapes.append(jax.ShapeDtypeStruct((n, rows), o.dtype))
        else:
            out_specs.append(pl.BlockSpec((tm, tn), lambda i, j: (i, j)))
            out_shapes.append(jax.ShapeDtypeStruct((rows, n), o.dtype))
    if narrow is not None:
        assert not narrow.transposed
        nn = narrow.w.shape[1]
        in_specs.append(pl.BlockSpec((k, nn), lambda i, j: (0, 0)))
        args.append(narrow.w)
        out_specs.append(pl.BlockSpec((tm, nn), lambda i, j: (i, 0)))
        out_shapes.append(jax.ShapeDtypeStruct((rows, nn), narrow.dtype))
    strip = lambda o: o._replace(w=None)
    return pl.pallas_call(
        functools.partial(_norm_proj_kernel, outs=tuple(strip(o) for o in outs),
                          narrow=None if narrow is None else strip(narrow),
                          shift_idx=shift_idx, scale_idx=scale_idx),
        grid=(rows // tm, n // tn),
        in_specs=in_specs,
        out_specs=out_specs,
        out_shape=out_shapes,
        scratch_shapes=[pltpu.VMEM((tm, k), BF16)],
        compiler_params=_cparams("parallel", "arbitrary"),
        name="norm_proj",
    )(*args)


def _out_proj_kernel(o_ref, w_ref, h_ref, g_ref, mod_ref, out_ref, *, gate_idx, o_transposed):
    y = _dot_tn(o_ref[...], w_ref[...]) if o_transposed else _dot(o_ref[...], w_ref[...])
    out_ref[...] = h_ref[...] + mod_ref[gate_idx:gate_idx + 1, :] * _rms(y, g_ref[...])


def out_proj_residual(o, w, h, g, mod, *, gate_idx, rows_per_mod, mod_base, o_transposed=False, tm=512):
    k, d = w.shape
    rows = h.shape[0]
    tm = min(tm, rows)
    o_spec = pl.BlockSpec((k, tm), lambda i: (0, i)) if o_transposed else pl.BlockSpec((tm, k), lambda i: (i, 0))
    return pl.pallas_call(
        functools.partial(_out_proj_kernel, gate_idx=gate_idx, o_transposed=o_transposed),
        grid=(rows // tm,),
        in_specs=[
            o_spec,
            pl.BlockSpec((k, d), lambda i: (0, 0)),
            pl.BlockSpec((tm, d), lambda i: (i, 0)),
            pl.BlockSpec((1, d), lambda i: (0, 0)),
            pl.BlockSpec((None, N_MOD, d), _mod_index_map(tm, rows_per_mod, mod_base)),
        ],
        out_specs=pl.BlockSpec((tm, d), lambda i: (i, 0)),
        out_shape=jax.ShapeDtypeStruct((rows, d), F32),
        compiler_params=_cparams("parallel"),
        name="out_proj_residual",
    )(o, w, h, g.reshape(1, d), mod)


def _ffn_kernel(x_ref, g1_ref, mod_ref, w1_ref, w2_ref, g2_ref, out_ref, a_scr,
                *, shift_idx, scale_idx, gate_idx):
    f = pl.program_id(1)
    last = pl.num_programs(1) - 1
    tm = x_ref.shape[0]
    n_slices = 2 if tm % 512 == 0 else 1
    ts = tm // n_slices

    def make_mlp():
        w1 = w1_ref[...].astype(BF16)
        w2 = w2_ref[...].astype(BF16)

        def mlp(a):
            hid = jnp.maximum(_dot(a, w1), 0.0)
            return _dot((hid * hid).astype(BF16), w2)
        return mlp

    @pl.when(f == 0)
    def _():
        mlp = make_mlp()
        for r in range(n_slices):
            rows = slice(r * ts, (r + 1) * ts)
            y = _rms(x_ref[rows, :], g1_ref[...])
            y = y * (1.0 + mod_ref[scale_idx:scale_idx + 1, :]) + mod_ref[shift_idx:shift_idx + 1, :]
            a = y.astype(BF16)
            a_scr[rows, :] = a
            out_ref[rows, :] = mlp(a)

    @pl.when(jnp.logical_and(f != 0, f != last))
    def _():
        out_ref[...] += make_mlp()(a_scr[...])

    @pl.when(jnp.logical_and(f != 0, f == last))
    def _():
        mlp = make_mlp()
        for r in range(n_slices):
            rows = slice(r * ts, (r + 1) * ts)
            y = out_ref[rows, :] + mlp(a_scr[rows, :])
            out_ref[rows, :] = x_ref[rows, :] + mod_ref[gate_idx:gate_idx + 1, :] * _rms(y, g2_ref[...])


def ffn_residual(x, g_pre, g_post, mod, w1_all, w2_all, layer, *, rows_per_mod, mod_base, tm=1024, tf=512):
    rows, d = x.shape
    dff = w1_all.shape[2]
    tm = min(tm, rows)
    assert dff // tf >= 2
    return pl.pallas_call(
        functools.partial(_ffn_kernel, shift_idx=3, scale_idx=4, gate_idx=5),
        grid=(rows // tm, dff // tf),
        in_specs=[
            pl.BlockSpec((tm, d), lambda i, f: (i, 0), pipeline_mode=pl.Buffered(1)),
            pl.BlockSpec((1, d), lambda i, f: (0, 0)),
            pl.BlockSpec((None, N_MOD, d), _mod_index_map(tm, rows_per_mod, mod_base)),
            pl.BlockSpec((None, d, tf), lambda i, f: (layer, 0, f)),
            pl.BlockSpec((None, tf, d), lambda i, f: (layer, f, 0)),
            pl.BlockSpec((1, d), lambda i, f: (0, 0)),
        ],
        out_specs=pl.BlockSpec((tm, d), lambda i, f: (i, 0)),
        out_shape=jax.ShapeDtypeStruct((rows, d), F32),
        scratch_shapes=[pltpu.VMEM((tm, d), BF16)],
        compiler_params=_cparams("parallel", "arbitrary"),
        name="ffn_residual",
    )(x, g_pre.reshape(1, d), mod, w1_all, w2_all, g_post.reshape(1, d))


def _ctx_attn_kernel(q_ref, k_ref, v_ref, o_ref, *, scale, base2):
    s = _dot_nt(q_ref[...], k_ref[...])
    if scale != 1.0:
        s = s * scale
    m = jnp.max(s, axis=-1, keepdims=True)
    p = jnp.exp2(s - m) if base2 else jnp.exp(s - m)
    l = jnp.sum(p, axis=-1, keepdims=True)
    o_ref[...] = (_dot(p.astype(BF16), v_ref[...]) / l).astype(o_ref.dtype)


def ctx_attention(q_arr, k_arr, v_arr, *, batch, heads, lc, dq, dv, q_off, k_off, v_off, scale, base2=False):
    return pl.pallas_call(
        functools.partial(_ctx_attn_kernel, scale=scale, base2=base2),
        grid=(batch, heads),
        in_specs=[
            pl.BlockSpec((lc, dq), lambda b, h: (b, q_off + h)),
            pl.BlockSpec((lc, dq), lambda b, h: (b, k_off + h)),
            pl.BlockSpec((lc, dv), lambda b, h: (b, v_off + h)),
        ],
        out_specs=pl.BlockSpec((lc, dv), lambda b, h: (b, h)),
        out_shape=jax.ShapeDtypeStruct((batch * lc, heads * dv), BF16),
        compiler_params=_cparams("parallel", "parallel"),
        name="ctx_attention",
    )(q_arr, k_arr, v_arr)


def _na_window_tables(img_rows):
    kr = min(NA_WIN_R, img_rows)
    n_blk = img_rows // NA_Q_ROWS

    def tables(i):
        ks = int(np.clip(i * NA_Q_ROWS - NA_WIN_R // 2, 0, img_rows - NA_K_ROWS))
        r = i * NA_Q_ROWS + np.arange(NA_Q_ROWS)[:, None]
        kr_abs = ks + np.arange(NA_K_ROWS)[None, :]
        rs = np.clip(r - kr // 2, 0, img_rows - kr)
        valid = (kr_abs >= rs) & (kr_abs < rs + kr)
        dr = np.clip(kr_abs - r + NA_WIN_R - 1, 0, 2 * NA_WIN_R - 2)
        return ks, valid, dr

    first, inner, last = tables(0), tables(1), tables(n_blk - 1)
    for i in range(1, n_blk - 1):
        ks, valid, dr = tables(i)
        assert ks == i * NA_Q_ROWS - NA_WIN_R // 2
        assert (valid == inner[1]).all() and (dr[valid] == inner[2][valid]).all()
    assert first[0] == 0 and last[0] == img_rows - NA_K_ROWS
    return [first, inner, last]


def _na_bias_tables(rpb, img_rows):
    qcol = np.arange(GRID_W)
    col_start = np.clip(qcol - NA_WIN_C // 2, 0, GRID_W - NA_WIN_C)
    col_mask = (qcol[None, :] >= col_start[:, None]) & (qcol[None, :] < col_start[:, None] + NA_WIN_C)
    dc_idx = np.clip(qcol[None, :] - qcol[:, None], -(NA_WIN_C - 1), NA_WIN_C - 1) + NA_WIN_C - 1
    rpb_cols = rpb[:, :, dc_idx]
    out = []
    for _, valid, dr in _na_window_tables(img_rows):
        t = rpb_cols[:, dr]
        mask = valid[:, :, None, None] & col_mask[None, None]
        t = jnp.where(mask[None], t * math.log2(math.e), NEG_BIG)
        t = t.transpose(0, 2, 4, 1, 3).reshape(rpb.shape[0], NA_K_ROWS * GRID_W, NA_Q_ROWS * GRID_W)
        out.append(t)
    return jnp.stack(out, axis=1)


def _na_kernel(q_ref, k_ref, vt_ref, kc_ref, vct_ref, bias_ref, ot_ref, *, img_rows):
    qb = NA_Q_ROWS * GRID_W
    kb = NA_K_ROWS * GRID_W
    n_blk = img_rows // NA_Q_ROWS
    kc = kc_ref[...]
    vct = vct_ref[...]

    def offsets(i):
        if isinstance(i, int):
            return i * qb, min(max(i * NA_Q_ROWS - NA_WIN_R // 2, 0), img_rows - NA_K_ROWS) * GRID_W
        q0 = pl.multiple_of(i * qb, qb)
        return q0, pl.multiple_of(q0 - (NA_WIN_R // 2) * GRID_W, qb)

    def logits(i, variant):
        q0, ks = offsets(i)
        q = q_ref[pl.ds(q0, qb), :]
        return _dot_nt(k_ref[pl.ds(ks, kb), :], q) + bias_ref[variant], _dot_nt(kc, q)

    def softmax(s_w, s_c):
        m = jnp.maximum(jnp.max(s_w, axis=0, keepdims=True), jnp.max(s_c, axis=0, keepdims=True))
        p_w = jnp.exp2(s_w - m)
        p_c = jnp.exp2(s_c - m)
        l = jnp.sum(p_w, axis=0, keepdims=True) + jnp.sum(p_c, axis=0, keepdims=True)
        return p_w.astype(BF16), p_c.astype(BF16), l

    def weighted_values(i, p_w, p_c, l):
        q0, ks = offsets(i)
        o = _dot(vt_ref[:, pl.ds(ks, kb)], p_w) + _dot(vct, p_c)
        ot_ref[:, pl.ds(q0, qb)] = (o / l).astype(ot_ref.dtype)

    def group(blocks):
        n = len(blocks)
        s, p = {}, {}
        for step in range(n + 2):
            if step < n:
                s[step] = logits(*blocks[step])
            if 1 <= step <= n:
                p[step - 1] = softmax(*s.pop(step - 1))
            if 2 <= step:
                weighted_values(blocks[step - 2][0], *p.pop(step - 2))

    group([(0, 0)] + [(i, 1) for i in range(1, n_blk - 1)] + [(n_blk - 1, 2)])


def na_attention(q_lat, k_lat, vt_lat, k_ctx, vt_ctx, bias, *, batch, seq, lc):
    heads = NA_HEADS
    dh = q_lat.shape[1] // heads
    img_rows = seq // GRID_W
    qb = NA_Q_ROWS * GRID_W
    kb = NA_K_ROWS * GRID_W
    assert (NA_WIN_R // 2) * GRID_W == qb
    return pl.pallas_call(
        functools.partial(_na_kernel, img_rows=img_rows),
        grid=(batch, heads),
        in_specs=[
            pl.BlockSpec((seq, dh), lambda b, h: (b, h)),
            pl.BlockSpec((seq, dh), lambda b, h: (b, h)),
            pl.BlockSpec((dh, seq), lambda b, h: (h, b)),
            pl.BlockSpec((lc, dh), lambda b, h: (b, h)),
            pl.BlockSpec((dh, lc), lambda b, h: (h, b)),
            pl.BlockSpec((None, 3, kb, qb), lambda b, h: (h, 0, 0, 0)),
        ],
        out_specs=pl.BlockSpec((dh, seq), lambda b, h: (h, b)),
        out_shape=jax.ShapeDtypeStruct((heads * dh, batch * seq), BF16),
        compiler_params=_cparams("parallel", "parallel"),
        name="na_attention",
    )(q_lat, k_lat, vt_lat, k_ctx, vt_ctx, bias)


def _mla_up_kernel(z_ref, gq_ref, gkv_ref, wq_ref, wkv_ref, cos_ref, sin_ref,
                   q_ref, k_ref, vt_ref, *maybe_v_ref, scale):
    hq = MLA_HEADS * MLA_QK_PAD
    z = z_ref[...]
    cq = _rms(z[:, :MLA_Q_RANK], gq_ref[...]).astype(BF16)
    ckv = _rms(z[:, MLA_Q_RANK:MLA_Q_RANK + MLA_KV_RANK], gkv_ref[...]).astype(BF16)
    cos = cos_ref[...]
    sin = sin_ref[...]
    c0 = MLA_Q_RANK + MLA_KV_RANK
    kpe = (z[:, c0:c0 + LANE] * cos + z[:, c0 + LANE:c0 + 2 * LANE] * sin).astype(BF16)
    qraw = _dot(cq, wq_ref[...])
    kv = _dot(ckv, wkv_ref[...])
    for h in range(MLA_HEADS):
        a = h * MLA_QK_PAD
        q_ref[:, a:a + LANE] = (qraw[:, a:a + LANE] * scale).astype(BF16)
        pe = qraw[:, a + LANE:a + 2 * LANE]
        sw = qraw[:, hq + h * LANE:hq + (h + 1) * LANE]
        q_ref[:, a + LANE:a + 2 * LANE] = ((pe * cos + sw * sin) * scale).astype(BF16)
        k_ref[:, a:a + LANE] = kv[:, h * MLA_NOPE:(h + 1) * MLA_NOPE].astype(BF16)
        k_ref[:, a + LANE:a + 2 * LANE] = kpe
    v = kv[:, MLA_HEADS * MLA_NOPE:]
    vt_ref[...] = v.T.astype(BF16)
    for v_ref in maybe_v_ref:
        v_ref[...] = v.astype(BF16)


def mla_up(z, g_q, g_kv, wq, wkv, cos, sin, *, with_v, tm=256):
    rows = z.shape[0]
    tm = min(tm, rows)
    n_pos = cos.shape[0] // tm
    hq = MLA_HEADS * MLA_QK_PAD
    hv = MLA_HEADS * MLA_V
    v_specs = [pl.BlockSpec((tm, hv), lambda i: (i, 0))] if with_v else []
    v_shapes = [jax.ShapeDtypeStruct((rows, hv), BF16)] if with_v else []
    return pl.pallas_call(
        functools.partial(_mla_up_kernel, scale=(MLA_NOPE + MLA_ROPE) ** -0.5 * math.log2(math.e)),
        grid=(rows // tm,),
        in_specs=[
            pl.BlockSpec((tm, z.shape[1]), lambda i: (i, 0)),
            pl.BlockSpec((1, MLA_Q_RANK), lambda i: (0, 0)),
            pl.BlockSpec((1, MLA_KV_RANK), lambda i: (0, 0)),
            pl.BlockSpec(wq.shape, lambda i: (0, 0)),
            pl.BlockSpec(wkv.shape, lambda i: (0, 0)),
            pl.BlockSpec((tm, LANE), lambda i: (i % n_pos, 0)),
            pl.BlockSpec((tm, LANE), lambda i: (i % n_pos, 0)),
        ],
        out_specs=[
            pl.BlockSpec((tm, hq), lambda i: (i, 0)),
            pl.BlockSpec((tm, hq), lambda i: (i, 0)),
            pl.BlockSpec((hv, tm), lambda i: (0, i)),
        ] + v_specs,
        out_shape=[
            jax.ShapeDtypeStruct((rows, hq), BF16),
            jax.ShapeDtypeStruct((rows, hq), BF16),
            jax.ShapeDtypeStruct((hv, rows), BF16),
        ] + v_shapes,
        compiler_params=_cparams("parallel"),
        name="mla_up",
    )(z, g_q.reshape(1, -1), g_kv.reshape(1, -1), wq, wkv, cos, sin)


def _mla_attn_kernel(q_ref, k_ref, vt_ref, kc_ref, vct_ref, ot_ref, *, tk):
    q = q_ref[...]
    n_tiles = 1 + k_ref.shape[0] // tk

    def logits(j):
        k = kc_ref[...] if j == 0 else k_ref[(j - 1) * tk:j * tk, :]
        return _dot_nt(k, q)

    def values(j):
        return vct_ref[...] if j == 0 else vt_ref[:, (j - 1) * tk:j * tk]

    m = l = acc = pending = None
    st = {0: logits(0)}
    for step in range(n_tiles + 1):
        if step + 1 < n_tiles:
            st[step + 1] = logits(step + 1)
        if step < n_tiles:
            s = st.pop(step)
            m_tile = jnp.max(s, axis=0, keepdims=True)
            m_new = m_tile if m is None else jnp.maximum(m, m_tile)
            alpha = None if m is None else jnp.exp2(m - m_new)
            p = jnp.exp2(s - m_new)
            m = m_new
            done_softmax = (step, alpha, p.astype(BF16), jnp.sum(p, axis=0, keepdims=True))
        if pending is not None:
            j, alpha_j, p_j, p_sum = pending
            pv = _dot(values(j), p_j)
            l, acc = (p_sum, pv) if acc is None else (alpha_j * l + p_sum, alpha_j * acc + pv)
        pending = done_softmax if step < n_tiles else None
    ot_ref[...] = (acc / l).astype(ot_ref.dtype)


def mla_attention(q_lat, k_lat, vt_lat, k_ctx, vt_ctx, *, batch, seq, lc, tq=512, tk=1024):
    heads = MLA_HEADS
    tq = min(tq, seq)
    tk = min(tk, seq)
    nq = seq // tq
    return pl.pallas_call(
        functools.partial(_mla_attn_kernel, tk=tk),
        grid=(batch, heads, nq),
        in_specs=[
            pl.BlockSpec((tq, MLA_QK_PAD), lambda b, h, i: (b * nq + i, h)),
            pl.BlockSpec((seq, MLA_QK_PAD), lambda b, h, i: (b, h)),
            pl.BlockSpec((MLA_V, seq), lambda b, h, i: (h, b)),
            pl.BlockSpec((lc, MLA_QK_PAD), lambda b, h, i: (b, h)),
            pl.BlockSpec((MLA_V, lc), lambda b, h, i: (h, b)),
        ],
        out_specs=pl.BlockSpec((MLA_V, tq), lambda b, h, i: (h, b * nq + i)),
        out_shape=jax.ShapeDtypeStruct((heads * MLA_V, batch * seq), BF16),
        compiler_params=_cparams("parallel", "parallel", "arbitrary"),
        name="mla_attention",
    )(q_lat, k_lat, vt_lat, k_ctx, vt_ctx)


def _rope_tables(seq):
    t = np.arange(seq)
    pos = np.stack([t // GRID_W, t % GRID_W], axis=-1).astype(np.float32)
    freqs = jnp.asarray(ROPE_BASE, F32) ** (-jnp.arange(ROPE_PAIRS, dtype=F32) / ROPE_PAIRS)
    ang = jnp.asarray(pos)[:, :, None] * freqs
    cos, sin = jnp.cos(ang), jnp.sin(ang)
    cos_t = jnp.stack([cos, cos], axis=2).reshape(seq, MLA_ROPE)
    sin_t = jnp.stack([-sin, sin], axis=2).reshape(seq, MLA_ROPE)
    pad = jnp.zeros((seq, LANE - MLA_ROPE), F32)
    return jnp.concatenate([cos_t, pad], axis=1), jnp.concatenate([sin_t, pad], axis=1)


_ROPE_SWAP = np.arange(MLA_ROPE).reshape(2, 2, ROPE_PAIRS)[:, ::-1, :].reshape(-1)


def _mla_weights(w_in, w_uq, w_ukv):
    d = w_in.shape[0]
    c0 = MLA_Q_RANK + MLA_KV_RANK
    zpad = jnp.zeros((d, LANE - MLA_ROPE), w_in.dtype)
    kpe = w_in[:, c0:]
    w_in_x = jnp.concatenate([w_in[:, :c0], kpe, zpad, kpe[:, _ROPE_SWAP], zpad], axis=1)
    r = w_uq.shape[0]
    wq = w_uq.reshape(r, MLA_HEADS, MLA_NOPE + MLA_ROPE)
    zq = jnp.zeros((r, MLA_HEADS, LANE - MLA_ROPE), w_uq.dtype)
    main = jnp.concatenate([wq, zq], axis=2).reshape(r, MLA_HEADS * MLA_QK_PAD)
    swapped = jnp.concatenate([wq[:, :, MLA_NOPE:][:, :, _ROPE_SWAP], zq], axis=2).reshape(r, MLA_HEADS * LANE)
    wq_x = jnp.concatenate([main, swapped], axis=1)
    wkv = w_ukv.reshape(w_ukv.shape[0], MLA_HEADS, MLA_NOPE + MLA_V)
    wkv_x = jnp.concatenate([wkv[:, :, :MLA_NOPE].reshape(r, -1), wkv[:, :, MLA_NOPE:].reshape(r, -1)], axis=1)
    return w_in_x.astype(BF16), wq_x.astype(BF16), wkv_x.astype(BF16)


def _ml_conv_kernel(prev_ref, x_ref, next_ref, w_ref, q_ref, k_ref, *, seg_len, scale):
    tm = x_ref.shape[0]
    hk = ML_HEADS * ML_QK
    x = x_ref[...]
    row = lax.broadcasted_iota(jnp.int32, (tm, 1), 0)
    pos = (pl.program_id(0) * tm) % seg_len + row
    x_prev = jnp.where(row == 0, prev_ref[7:8, :], pltpu.roll(x, 1, 0))
    x_next = jnp.where(row == tm - 1, next_ref[0:1, :], pltpu.roll(x, tm - 1, 0))
    x_prev = jnp.where(pos == 0, 0.0, x_prev)
    x_next = jnp.where(pos == seg_len - 1, 0.0, x_next)
    y = x_prev * w_ref[0:1, :] + x * w_ref[1:2, :] + x_next * w_ref[2:3, :]
    y = y * jax.nn.sigmoid(y)
    q_ref[...] = (y[:, :hk] * scale).astype(BF16)
    k_ref[...] = y[:, hk:].astype(BF16)


def ml_conv_silu(qk_raw, conv_w, *, seg_len, tm=256):
    rows, c = qk_raw.shape
    tm = min(tm, seg_len)
    hk = ML_HEADS * ML_QK
    nb8 = rows // 8
    t8 = tm // 8
    return pl.pallas_call(
        functools.partial(_ml_conv_kernel, seg_len=seg_len, scale=ML_QK ** -0.5),
        grid=(rows // tm,),
        in_specs=[
            pl.BlockSpec((8, c), lambda i: (jnp.maximum(i * t8 - 1, 0), 0)),
            pl.BlockSpec((tm, c), lambda i: (i, 0)),
            pl.BlockSpec((8, c), lambda i: (jnp.minimum((i + 1) * t8, nb8 - 1), 0)),
            pl.BlockSpec((ML_CONV, c), lambda i: (0, 0)),
        ],
        out_specs=[pl.BlockSpec((tm, hk), lambda i: (i, 0)), pl.BlockSpec((tm, hk), lambda i: (i, 0))],
        out_shape=[jax.ShapeDtypeStruct((rows, hk), BF16), jax.ShapeDtypeStruct((rows, hk), BF16)],
        compiler_params=_cparams("parallel"),
        name="ml_conv_silu",
    )(qk_raw, qk_raw, qk_raw, conv_w)


def _log_sigmoid(x):
    return jnp.minimum(x, 0.0) - jnp.log(1.0 + jnp.exp(-jnp.abs(x)))


def _running_sum(x, axis, reverse):
    n = x.shape[axis]
    idx = lax.broadcasted_iota(jnp.int32, x.shape, axis)
    shift = 1
    while shift < n:
        if reverse:
            x = x + jnp.where(idx < n - shift, pltpu.roll(x, n - shift, axis), 0.0)
        else:
            x = x + jnp.where(idx >= shift, pltpu.roll(x, shift, axis), 0.0)
        shift *= 2
    return x


ML_STATE_ROWS = ML_V + 16


def _ml_scan_kernel(q_ref, k_ref, vt_ref, g_ref, gt_ref, b_ref, bt_ref, cn0_ref, m0_ref,
                    ht_ref, cn_out, m_out, cn_scr, m_scr, *, reverse):
    step = pl.program_id(1)
    tc = q_ref.shape[0]

    @pl.when(step == 0)
    def _():
        cn_scr[...] = cn0_ref[...]
        m_scr[...] = m0_ref[...]

    s_idx = lax.broadcasted_iota(jnp.int32, (tc, tc), 0)
    t_idx = lax.broadcasted_iota(jnp.int32, (tc, tc), 1)
    seen = (s_idx >= t_idx) if reverse else (s_idx <= t_idx)
    gi = 2 * ML_HEADS if reverse else 0
    gf = gi + ML_HEADS

    g = g_ref[...] + b_ref[...]
    gt = gt_ref[...] + bt_ref[...]
    lsg_t = _log_sigmoid(gt)
    cum_c = _running_sum(_log_sigmoid(g), 0, reverse)
    cum_r = _running_sum(lsg_t, 1, reverse)

    for hd in range(ML_HEADS):
        qh = q_ref[:, hd * ML_QK:(hd + 1) * ML_QK]
        kh = k_ref[:, hd * ML_QK:(hd + 1) * ML_QK]
        vth = vt_ref[hd * ML_V:(hd + 1) * ML_V, :]
        b_r = cum_r[gf + hd:gf + hd + 1, :]
        b_end = jnp.sum(lsg_t[gf + hd:gf + hd + 1, :], axis=1, keepdims=True)
        col = g[:, gi + hd:gi + hd + 1] - cum_c[:, gf + hd:gf + hd + 1]
        m_prev = m_scr[hd:hd + 1, 0:1]
        cn_prev = cn_scr[hd]

        log_w = jnp.where(seen, b_r + col, NEG_BIG)
        log_inter = b_r + m_prev
        m_t = jnp.maximum(log_inter, jnp.max(log_w, axis=0, keepdims=True))
        w_inter = jnp.exp(log_inter - m_t)
        st = _dot_nt(kh, qh) * jnp.exp(log_w - m_t)
        inter = _dot_nt(cn_prev.astype(BF16), qh)
        num = w_inter * inter[:ML_V, :] + _dot(vth, st.astype(BF16))
        den = w_inter * inter[ML_V:ML_V + 1, :] + jnp.sum(st, axis=0, keepdims=True)
        ht_ref[hd * ML_V:(hd + 1) * ML_V, :] = num / jnp.maximum(jnp.abs(den), jnp.exp(-m_t))

        log_s = col + b_end
        m_new = jnp.maximum(b_end + m_prev, jnp.max(log_s, axis=0, keepdims=True))
        decay = jnp.exp(b_end + m_prev - m_new)
        kw = kh.astype(F32) * jnp.exp(log_s - m_new)
        cn_scr[hd, :ML_V, :] = decay * cn_prev[:ML_V, :] + _dot(vth, kw.astype(BF16))
        cn_scr[hd, ML_V:ML_V + 1, :] = decay * cn_prev[ML_V:ML_V + 1, :] + jnp.sum(kw, axis=0, keepdims=True)
        m_scr[hd:hd + 1, :] = jnp.broadcast_to(m_new, (1, LANE))

    @pl.when(step == pl.num_programs(1) - 1)
    def _():
        cn_out[...] = cn_scr[...]
        m_out[...] = m_scr[...]


def ml_scan(q, k, vt, g, gt, b_row, b_col, state, *, batch, seg_len, reverse):
    tc = min(ML_CHUNK, seg_len)
    nc = seg_len // tc
    hk = ML_HEADS * ML_QK
    hv = ML_HEADS * ML_V
    cn0, m0 = state

    def blk(b, s):
        return b * nc + ((nc - 1 - s) if reverse else s)

    state_specs = [
        pl.BlockSpec((None, ML_HEADS, ML_STATE_ROWS, ML_QK), lambda b, s: (b, 0, 0, 0)),
        pl.BlockSpec((None, ML_HEADS, LANE), lambda b, s: (b, 0, 0)),
    ]
    return pl.pallas_call(
        functools.partial(_ml_scan_kernel, reverse=reverse),
        grid=(batch, nc),
        in_specs=[
            pl.BlockSpec((tc, hk), lambda b, s: (blk(b, s), 0)),
            pl.BlockSpec((tc, hk), lambda b, s: (blk(b, s), 0)),
            pl.BlockSpec((hv, tc), lambda b, s: (0, blk(b, s))),
            pl.BlockSpec((tc, LANE), lambda b, s: (blk(b, s), 0)),
            pl.BlockSpec((4 * ML_HEADS, tc), lambda b, s: (0, blk(b, s))),
            pl.BlockSpec((1, LANE), lambda b, s: (0, 0)),
            pl.BlockSpec((4 * ML_HEADS, 1), lambda b, s: (0, 0)),
        ] + state_specs,
        out_specs=[pl.BlockSpec((hv, tc), lambda b, s: (0, blk(b, s)))] + state_specs,
        out_shape=[
            jax.ShapeDtypeStruct((hv, batch * seg_len), F32),
            jax.ShapeDtypeStruct(cn0.shape, F32),
            jax.ShapeDtypeStruct(m0.shape, F32),
        ],
        scratch_shapes=[
            pltpu.VMEM((ML_HEADS, ML_STATE_ROWS, ML_QK), F32),
            pltpu.VMEM((ML_HEADS, LANE), F32),
        ],
        compiler_params=_cparams("parallel", "arbitrary"),
        name="ml_scan_bwd" if reverse else "ml_scan_fwd",
    )(q, k, vt, g, gt, b_row, b_col, cn0, m0)


def _ml_out_kernel(hf_ref, hb_ref, og_ref, gh_ref, w_ref, h_ref, g_ref, mod_ref, out_ref, x_scr, *, gate_idx):
    for hd in range(ML_HEADS):
        sl = slice(hd * ML_V, (hd + 1) * ML_V)
        hh = hf_ref[sl, :] + hb_ref[sl, :]
        hn = hh * lax.rsqrt(jnp.mean(hh * hh, axis=0, keepdims=True) + EPS) * gh_ref[sl, :]
        x_scr[sl, :] = (jax.nn.sigmoid(og_ref[sl, :]) * hn).astype(BF16)
    y = _dot_tn(x_scr[...], w_ref[...])
    out_ref[...] = h_ref[...] + mod_ref[gate_idx:gate_idx + 1, :] * _rms(y, g_ref[...])


def ml_out_residual(ht_f, ht_b, ogt, g_head, w, h, g, mod, *, gate_idx, rows_per_mod, mod_base, tm=256):
    hv, rows = ht_f.shape
    d = w.shape[1]
    tm = min(tm, rows)
    return pl.pallas_call(
        functools.partial(_ml_out_kernel, gate_idx=gate_idx),
        grid=(rows // tm,),
        in_specs=[
            pl.BlockSpec((hv, tm), lambda i: (0, i)),
            pl.BlockSpec((hv, tm), lambda i: (0, i)),
            pl.BlockSpec((hv, tm), lambda i: (0, i)),
            pl.BlockSpec((hv, 1), lambda i: (0, 0)),
            pl.BlockSpec((hv, d), lambda i: (0, 0)),
            pl.BlockSpec((tm, d), lambda i: (i, 0)),
            pl.BlockSpec((1, d), lambda i: (0, 0)),
            pl.BlockSpec((None, N_MOD, d), _mod_index_map(tm, rows_per_mod, mod_base)),
        ],
        out_specs=pl.BlockSpec((tm, d), lambda i: (i, 0)),
        out_shape=jax.ShapeDtypeStruct((rows, d), F32),
        scratch_shapes=[pltpu.VMEM((hv, tm), BF16)],
        compiler_params=_cparams("parallel"),
        name="ml_out_residual",
    )(ht_f, ht_b, ogt, g_head.reshape(hv, 1), w, h, g.reshape(1, d), mod)


def _streams(batch, seq, lc):
    return dict(rows_per_mod=seq, mod_base=0), dict(rows_per_mod=batch * lc, mod_base=batch)


def na_mixer(h_lat, h_ctx, mod, g_pre, g_post, w_qkv, rpb, w_o, *, batch, seq, lc, need_ctx):
    lat, ctx = _streams(batch, seq, lc)
    hd = w_o.shape[0]
    dh = hd // NA_HEADS
    w = w_qkv.astype(BF16)
    wo = w_o.astype(BF16)
    q_out = ProjOut(w, BF16, scale=dh ** -0.5 * math.log2(math.e))
    k_out = ProjOut(w, BF16, col0=hd)
    v_out = ProjOut(w, BF16, col0=2 * hd)
    vt_out = v_out._replace(transposed=True)
    proj = functools.partial(norm_proj, g=g_pre, mod=mod, shift_idx=0, scale_idx=1, tn=512, n=hd)
    q_lat, k_lat, vt_lat = proj(h_lat, outs=[q_out, k_out, vt_out], tm=1024, **lat)
    ctx_outs = [q_out, k_out, vt_out] + ([v_out] if need_ctx else [])
    q_ctx, k_ctx, vt_ctx, *v_ctx = proj(h_ctx, outs=ctx_outs, tm=512, **ctx)
    bias = _na_bias_tables(rpb, seq // GRID_W)
    ot_lat = na_attention(q_lat, k_lat, vt_lat, k_ctx, vt_ctx, bias, batch=batch, seq=seq, lc=lc)
    h_lat = out_proj_residual(ot_lat, wo, h_lat, g_post, mod, gate_idx=2, o_transposed=True, **lat)
    if need_ctx:
        o_ctx = ctx_attention(q_ctx, k_ctx, v_ctx[0], batch=batch, heads=NA_HEADS, lc=lc, dq=dh, dv=dh,
                              q_off=0, k_off=0, v_off=0, scale=1.0, base2=True)
        h_ctx = out_proj_residual(o_ctx, wo, h_ctx, g_post, mod, gate_idx=2, **ctx)
    return h_lat, h_ctx


def mla_mixer(h_lat, h_ctx, mod, g_pre, g_post, w_in, g_q, g_kv, w_uq, w_ukv, w_o, *, batch, seq, lc, need_ctx):
    lat, ctx = _streams(batch, seq, lc)
    w_in_x, wq_x, wkv_x = _mla_weights(w_in, w_uq, w_ukv)
    wo = w_o.astype(BF16)
    proj = functools.partial(norm_proj, g=g_pre, outs=[ProjOut(w_in_x, F32)], mod=mod, shift_idx=0, scale_idx=1,
                             tn=w_in_x.shape[1])
    z_lat, = proj(h_lat, tm=1024, **lat)
    z_ctx, = proj(h_ctx, tm=512, **ctx)
    cos_l, sin_l = _rope_tables(seq)
    ones = jnp.concatenate([jnp.ones((lc, MLA_ROPE), F32), jnp.zeros((lc, LANE - MLA_ROPE), F32)], axis=1)
    q_lat, k_lat, vt_lat = mla_up(z_lat, g_q, g_kv, wq_x, wkv_x, cos_l, sin_l, with_v=False)
    q_ctx, k_ctx, vt_ctx, v_ctx = mla_up(z_ctx, g_q, g_kv, wq_x, wkv_x, ones, jnp.zeros_like(ones), with_v=True)
    ot_lat = mla_attention(q_lat, k_lat, vt_lat, k_ctx, vt_ctx, batch=batch, seq=seq, lc=lc)
    h_lat = out_proj_residual(ot_lat, wo, h_lat, g_post, mod, gate_idx=2, o_transposed=True, **lat)
    if need_ctx:
        o_ctx = ctx_attention(q_ctx, k_ctx, v_ctx, batch=batch, heads=MLA_HEADS, lc=lc, dq=MLA_QK_PAD, dv=MLA_V,
                              q_off=0, k_off=0, v_off=0, scale=1.0, base2=True)
        h_ctx = out_proj_residual(o_ctx, wo, h_ctx, g_post, mod, gate_idx=2, **ctx)
    return h_lat, h_ctx


def mlstm_mixer(h_lat, h_ctx, mod, g_pre, g_post, w_in, b_gate, conv_w, g_head, w_o, *, batch, seq, lc, need_ctx):
    lat, ctx = _streams(batch, seq, lc)
    d = w_in.shape[0]
    hk2 = 2 * ML_HEADS * ML_QK
    hv = ML_HEADS * ML_V
    ng = 4 * ML_HEADS
    assert hk2 == hv
    w_b = w_in.astype(BF16)
    w_g = jnp.concatenate([w_in[:, hk2 + 2 * hv:], jnp.zeros((d, LANE - ng), w_in.dtype)], axis=1).astype(BF16)
    wo = w_o.astype(BF16)
    b_row = jnp.concatenate([b_gate, jnp.zeros((LANE - ng,), F32)]).reshape(1, LANE)
    b_col = b_gate.reshape(ng, 1)

    def project(h, stream, seg_len):
        outs = [ProjOut(w_b, F32), ProjOut(w_b, BF16, transposed=True, col0=hk2),
                ProjOut(w_b, F32, transposed=True, col0=hk2 + hv)]
        qk_raw, vt, ogt, g = norm_proj(h, g_pre, outs, n=hv, narrow=ProjOut(w_g, F32),
                                       mod=mod, shift_idx=0, scale_idx=1, tm=512, tn=512, **stream)
        q, k = ml_conv_silu(qk_raw, conv_w, seg_len=seg_len)
        return q, k, vt, ogt, g, g[:, :ng].T

    def bidir(q, k, vt, g, gt, st_f, st_b, seg_len):
        h_f, *st_f = ml_scan(q, k, vt, g, gt, b_row, b_col, st_f, batch=batch, seg_len=seg_len, reverse=False)
        h_b, *st_b = ml_scan(q, k, vt, g, gt, b_row, b_col, st_b, batch=batch, seg_len=seg_len, reverse=True)
        return h_f, h_b, st_f, st_b

    zero = (jnp.zeros((batch, ML_HEADS, ML_STATE_ROWS, ML_QK), F32), jnp.zeros((batch, ML_HEADS, LANE), F32))
    qc, kc, vc, ogc, gc, gtc = project(h_ctx, ctx, lc)
    ql, kl, vl, ogl, gl, gtl = project(h_lat, lat, seq)
    hc_f, hc_b, st_f, st_b = bidir(qc, kc, vc, gc, gtc, zero, zero, lc)
    hl_f, hl_b, _, _ = bidir(ql, kl, vl, gl, gtl, st_f, st_b, seq)
    h_lat = ml_out_residual(hl_f, hl_b, ogl, g_head, wo, h_lat, g_post, mod, gate_idx=2, **lat)
    if need_ctx:
        h_ctx = ml_out_residual(hc_f, hc_b, ogc, g_head, wo, h_ctx, g_post, mod, gate_idx=2, **ctx)
    return h_lat, h_ctx


def kernel(x, c, ctx, c_ctx, ada_w, ada_b, norm_g, ff_w1, ff_w2, na_w_qkv, na_rpb, na_w_o, mla_w_in, mla_g_q, mla_g_kv, mla_w_uq, mla_w_ukv, mla_w_o, ml_w_in, ml_b_gate, ml_conv, ml_g_head, ml_w_o):
    batch, seq, d = x.shape
    lc = ctx.shape[1]
    depth = ada_w.shape[0]
    lat, cst = _streams(batch, seq, lc)

    n_cond = 8 * ((batch + 1 + 7) // 8)
    c_all = jnp.concatenate([c, c_ctx[None, :], jnp.zeros((n_cond - batch - 1, d), F32)], axis=0)
    mods = ada_modulation(c_all, ada_w, ada_b).reshape(depth, n_cond, N_MOD, d)

    h_lat = x.reshape(batch * seq, d)
    h_ctx = ctx.reshape(batch * lc, d)
    for i in range(depth):
        last = i == depth - 1
        mod = mods[i]
        g_pre1, g_post1, g_pre2, g_post2 = norm_g[i]
        kind, j = i % 3, i // 3
        dims = dict(batch=batch, seq=seq, lc=lc, need_ctx=not last)
        if kind == 0:
            h_lat, h_ctx = na_mixer(h_lat, h_ctx, mod, g_pre1, g_post1, na_w_qkv[j], na_rpb[j], na_w_o[j], **dims)
        elif kind == 1:
            h_lat, h_ctx = mla_mixer(h_lat, h_ctx, mod, g_pre1, g_post1, mla_w_in[j], mla_g_q[j], mla_g_kv[j],
                                     mla_w_uq[j], mla_w_ukv[j], mla_w_o[j], **dims)
        else:
            h_lat, h_ctx = mlstm_mixer(h_lat, h_ctx, mod, g_pre1, g_post1, ml_w_in[j], ml_b_gate[j], ml_conv[j],
                                       ml_g_head[j], ml_w_o[j], **dims)
        h_lat = ffn_residual(h_lat, g_pre2, g_post2, mod, ff_w1, ff_w2, i, **lat)
        if not last:
            h_ctx = ffn_residual(h_ctx, g_pre2, g_post2, mod, ff_w1, ff_w2, i, **cst)
    return h_lat.reshape(batch, seq, d)
```

```python
import functools
import math
from typing import Any, NamedTuple

import numpy as np
import jax
import jax.numpy as jnp
from jax import lax
from jax.experimental import pallas as pl
from jax.experimental.pallas import tpu as pltpu

F32 = jnp.float32
BF16 = jnp.bfloat16

EPS = 1e-6
N_MOD = 6
GRID_W = 64

NA_HEADS = 16
NA_WIN_R = 8
NA_WIN_C = 16
NA_Q_ROWS = 4
NA_K_ROWS = 12

MLA_HEADS = 16
MLA_Q_RANK = 512
MLA_KV_RANK = 512
MLA_NOPE = 128
MLA_ROPE = 64
MLA_V = 128
ROPE_PAIRS = MLA_ROPE // 4
ROPE_BASE = 10000.0
MLA_QK_PAD = 256

ML_HEADS = 8
ML_QK = 128
ML_V = 256
ML_CONV = 3
ML_CHUNK = 256

LANE = 128
NEG_BIG = -1e30
VMEM_LIMIT_BYTES = 56 * 1024 * 1024
FFN_VMEM_LIMIT_BYTES = 62 * 1024 * 1024


def _cparams(*sem):
    return pltpu.CompilerParams(dimension_semantics=sem, vmem_limit_bytes=VMEM_LIMIT_BYTES)


def _rms(x, g):
    return x * lax.rsqrt(jnp.mean(x * x, axis=-1, keepdims=True) + EPS) * g


def _dot(a, b):
    return jnp.dot(a, b, preferred_element_type=F32)


def _dot_nt(a, b):
    return lax.dot_general(a, b, (((1,), (1,)), ((), ())), preferred_element_type=F32)


def _dot_tn(a, b):
    return lax.dot_general(a, b, (((0,), (0,)), ((), ())), preferred_element_type=F32)


def _mod_index_map(tm, rows_per_mod, mod_base):
    return lambda i, *_: (mod_base + (i * tm) // rows_per_mod, 0, 0)


def _ada_kernel(c_ref, w_ref, b_ref, o_ref):
    c = c_ref[...]
    s = c * jax.nn.sigmoid(c)
    o_ref[...] = _dot(s, w_ref[...]) + b_ref[...]


def ada_modulation(c_all, ada_w, ada_b, *, tn=1024):
    depth, d, n = ada_w.shape
    r = c_all.shape[0]
    return pl.pallas_call(
        _ada_kernel,
        grid=(depth, n // tn),
        in_specs=[
            pl.BlockSpec((r, d), lambda l, j: (0, 0)),
            pl.BlockSpec((None, d, tn), lambda l, j: (l, 0, j)),
            pl.BlockSpec((None, 1, tn), lambda l, j: (l, 0, j)),
        ],
        out_specs=pl.BlockSpec((None, r, tn), lambda l, j: (l, 0, j)),
        out_shape=jax.ShapeDtypeStruct((depth, r, n), F32),
        compiler_params=_cparams("arbitrary", "arbitrary"),
        name="ada_modulation",
    )(c_all, ada_w, ada_b.reshape(depth, 1, n))


class ProjOut(NamedTuple):
    w: jax.Array
    dtype: Any
    transposed: bool = False
    scale: float = 1.0
    col0: int = 0


def _norm_proj_kernel(*refs, outs, narrow, shift_idx, scale_idx):
    n_w = len(outs) + (narrow is not None)
    n_in = 2 + (shift_idx is not None)
    x_ref, g_ref = refs[:2]
    mod_ref = refs[2] if shift_idx is not None else None
    w_refs = refs[n_in:n_in + n_w]
    o_refs = refs[n_in + n_w:n_in + 2 * n_w]
    a_scr = refs[-1]

    def emit(o_ref, w_ref, spec):
        y = _dot(a_scr[...], w_ref[...])
        if spec.scale != 1.0:
            y = y * spec.scale
        o_ref[...] = (y.T if spec.transposed else y).astype(o_ref.dtype)

    @pl.when(pl.program_id(1) == 0)
    def _():
        y = _rms(x_ref[...], g_ref[...])
        if shift_idx is not None:
            y = y * (1.0 + mod_ref[scale_idx:scale_idx + 1, :]) + mod_ref[shift_idx:shift_idx + 1, :]
        a_scr[...] = y.astype(BF16)
        if narrow is not None:
            emit(o_refs[-1], w_refs[-1], narrow)

    for o_ref, w_ref, spec in zip(o_refs, w_refs, outs):
        emit(o_ref, w_ref, spec)


def norm_proj(x, g, outs, *, tm, tn, n=None, narrow=None, mod=None, rows_per_mod=None, mod_base=0,
              shift_idx=None, scale_idx=None, x_col_block=0):
    rows = x.shape[0]
    k = outs[0].w.shape[0]
    n = outs[0].w.shape[1] if n is None else n
    tm = min(tm, rows)
    tn = min(tn, n)
    assert all(o.w.shape[0] == k and o.col0 % tn == 0 and o.col0 + n <= o.w.shape[1] for o in outs)
    in_specs = [
        pl.BlockSpec((tm, k), lambda i, j: (i, x_col_block)),
        pl.BlockSpec((1, k), lambda i, j: (0, 0)),
    ]
    args = [x, g.reshape(1, k)]
    if shift_idx is not None:
        in_specs.append(pl.BlockSpec((None, N_MOD, k), _mod_index_map(tm, rows_per_mod, mod_base)))
        args.append(mod)
    out_specs, out_shapes = [], []
    for o in outs:
        in_specs.append(pl.BlockSpec((k, tn), lambda i, j, _b0=o.col0 // tn: (0, _b0 + j)))
        args.append(o.w)
        if o.transposed:
            out_specs.append(pl.BlockSpec((tn, tm), lambda i, j: (j, i)))
            out_shapes.append(jax.ShapeDtypeStruct((n, rows), o.dtype))
        else:
            out_specs.append(pl.BlockSpec((tm, tn), lambda i, j: (i, j)))
            out_shapes.append(jax.ShapeDtypeStruct((rows, n), o.dtype))
    if narrow is not None:
        assert not narrow.transposed
        nn = narrow.w.shape[1]
        in_specs.append(pl.BlockSpec((k, nn), lambda i, j: (0, 0)))
        args.append(narrow.w)
        out_specs.append(pl.BlockSpec((tm, nn), lambda i, j: (i, 0)))
        out_shapes.append(jax.ShapeDtypeStruct((rows, nn), narrow.dtype))
    strip = lambda o: o._replace(w=None)
    return pl.pallas_call(
        functools.partial(_norm_proj_kernel, outs=tuple(strip(o) for o in outs),
                          narrow=None if narrow is None else strip(narrow),
                          shift_idx=shift_idx, scale_idx=scale_idx),
        grid=(rows // tm, n // tn),
        in_specs=in_specs,
        out_specs=out_specs,
        out_shape=out_shapes,
        scratch_shapes=[pltpu.VMEM((tm, k), BF16)],
        compiler_params=_cparams("parallel", "arbitrary"),
        name="norm_proj",
    )(*args)


def _out_proj_kernel(o_ref, w_ref, h_ref, g_ref, mod_ref, out_ref, *, gate_idx, o_transposed):
    y = _dot_tn(o_ref[...], w_ref[...]) if o_transposed else _dot(o_ref[...], w_ref[...])
    out_ref[...] = h_ref[...] + mod_ref[gate_idx:gate_idx + 1, :] * _rms(y, g_ref[...])


def out_proj_residual(o, w, h, g, mod, *, gate_idx, rows_per_mod, mod_base, o_transposed=False, tm=512):
    k, d = w.shape
    rows = h.shape[0]
    tm = min(tm, rows)
    o_spec = pl.BlockSpec((k, tm), lambda i: (0, i)) if o_transposed else pl.BlockSpec((tm, k), lambda i: (i, 0))
    return pl.pallas_call(
        functools.partial(_out_proj_kernel, gate_idx=gate_idx, o_transposed=o_transposed),
        grid=(rows // tm,),
        in_specs=[
            o_spec,
            pl.BlockSpec((k, d), lambda i: (0, 0)),
            pl.BlockSpec((tm, d), lambda i: (i, 0)),
            pl.BlockSpec((1, d), lambda i: (0, 0)),
            pl.BlockSpec((None, N_MOD, d), _mod_index_map(tm, rows_per_mod, mod_base)),
        ],
        out_specs=pl.BlockSpec((tm, d), lambda i: (i, 0)),
        out_shape=jax.ShapeDtypeStruct((rows, d), F32),
        compiler_params=_cparams("parallel"),
        name="out_proj_residual",
    )(o, w, h, g.reshape(1, d), mod)


def _ffn_kernel(x_ref, g1_ref, mod_ref, w1_ref, w2_ref, g2_ref, out_ref, a_scr,
                *, shift_idx, scale_idx, gate_idx):
    f = pl.program_id(1)
    last = pl.num_programs(1) - 1
    tm = x_ref.shape[0]
    n_slices = 2 if tm % 512 == 0 else 1
    ts = tm // n_slices

    def make_mlp():
        w1 = w1_ref[...].astype(BF16)
        w2 = w2_ref[...].astype(BF16)

        def mlp(a):
            hid = jnp.maximum(_dot(a, w1), 0.0)
            return _dot((hid * hid).astype(BF16), w2)
        return mlp

    @pl.when(f == 0)
    def _():
        mlp = make_mlp()
        for r in range(n_slices):
            rows = slice(r * ts, (r + 1) * ts)
            y = _rms(x_ref[rows, :], g1_ref[...])
            y = y * (1.0 + mod_ref[scale_idx:scale_idx + 1, :]) + mod_ref[shift_idx:shift_idx + 1, :]
            a = y.astype(BF16)
            a_scr[rows, :] = a
            out_ref[rows, :] = mlp(a)

    @pl.when(jnp.logical_and(f != 0, f != last))
    def _():
        out_ref[...] += make_mlp()(a_scr[...])

    @pl.when(jnp.logical_and(f != 0, f == last))
    def _():
        mlp = make_mlp()
        for r in range(n_slices):
            rows = slice(r * ts, (r + 1) * ts)
            y = out_ref[rows, :] + mlp(a_scr[rows, :])
            out_ref[rows, :] = x_ref[rows, :] + mod_ref[gate_idx:gate_idx + 1, :] * _rms(y, g2_ref[...])


def ffn_residual(x, g_pre, g_post, mod, w1_all, w2_all, layer, *, rows_per_mod, mod_base, tm=1024, tf=512):
    rows, d = x.shape
    dff = w1_all.shape[2]
    tm = min(tm, rows)
    assert dff // tf >= 2
    return pl.pallas_call(
        functools.partial(_ffn_kernel, shift_idx=3, scale_idx=4, gate_idx=5),
        grid=(rows // tm, dff // tf),
        in_specs=[
            pl.BlockSpec((tm, d), lambda i, f: (i, 0)),
            pl.BlockSpec((1, d), lambda i, f: (0, 0)),
            pl.BlockSpec((None, N_MOD, d), _mod_index_map(tm, rows_per_mod, mod_base)),
            pl.BlockSpec((None, d, tf), lambda i, f: (layer, 0, f)),
            pl.BlockSpec((None, tf, d), lambda i, f: (layer, f, 0)),
            pl.BlockSpec((1, d), lambda i, f: (0, 0)),
        ],
        out_specs=pl.BlockSpec((tm, d), lambda i, f: (i, 0)),
        out_shape=jax.ShapeDtypeStruct((rows, d), F32),
        scratch_shapes=[pltpu.VMEM((tm, d), BF16)],
        compiler_params=pltpu.CompilerParams(dimension_semantics=("parallel", "arbitrary"),
                                             vmem_limit_bytes=FFN_VMEM_LIMIT_BYTES),
        name="ffn_residual",
    )(x, g_pre.reshape(1, d), mod, w1_all, w2_all, g_post.reshape(1, d))


def _ctx_attn_kernel(q_ref, k_ref, v_ref, o_ref, *, scale, base2):
    s = _dot_nt(q_ref[...], k_ref[...])
    if scale != 1.0:
        s = s * scale
    m = jnp.max(s, axis=-1, keepdims=True)
    p = jnp.exp2(s - m) if base2 else jnp.exp(s - m)
    l = jnp.sum(p, axis=-1, keepdims=True)
    o_ref[...] = (_dot(p.astype(BF16), v_ref[...]) / l).astype(o_ref.dtype)


def ctx_attention(q_arr, k_arr, v_arr, *, batch, heads, lc, dq, dv, q_off, k_off, v_off, scale, base2=False):
    return pl.pallas_call(
        functools.partial(_ctx_attn_kernel, scale=scale, base2=base2),
        grid=(batch, heads),
        in_specs=[
            pl.BlockSpec((lc, dq), lambda b, h: (b, q_off + h)),
            pl.BlockSpec((lc, dq), lambda b, h: (b, k_off + h)),
            pl.BlockSpec((lc, dv), lambda b, h: (b, v_off + h)),
        ],
        out_specs=pl.BlockSpec((lc, dv), lambda b, h: (b, h)),
        out_shape=jax.ShapeDtypeStruct((batch * lc, heads * dv), BF16),
        compiler_params=_cparams("parallel", "parallel"),
        name="ctx_attention",
    )(q_arr, k_arr, v_arr)


def _na_window_tables(img_rows):
    kr = min(NA_WIN_R, img_rows)
    n_blk = img_rows // NA_Q_ROWS

    def tables(i):
        ks = int(np.clip(i * NA_Q_ROWS - NA_WIN_R // 2, 0, img_rows - NA_K_ROWS))
        r = i * NA_Q_ROWS + np.arange(NA_Q_ROWS)[:, None]
        kr_abs = ks + np.arange(NA_K_ROWS)[None, :]
        rs = np.clip(r - kr // 2, 0, img_rows - kr)
        valid = (kr_abs >= rs) & (kr_abs < rs + kr)
        dr = np.clip(kr_abs - r + NA_WIN_R - 1, 0, 2 * NA_WIN_R - 2)
        return ks, valid, dr

    first, inner, last = tables(0), tables(1), tables(n_blk - 1)
    for i in range(1, n_blk - 1):
        ks, valid, dr = tables(i)
        assert ks == i * NA_Q_ROWS - NA_WIN_R // 2
        assert (valid == inner[1]).all() and (dr[valid] == inner[2][valid]).all()
    assert first[0] == 0 and last[0] == img_rows - NA_K_ROWS
    return [first, inner, last]


def _na_bias_tables(rpb, img_rows):
    qcol = np.arange(GRID_W)
    col_start = np.clip(qcol - NA_WIN_C // 2, 0, GRID_W - NA_WIN_C)
    col_mask = (qcol[None, :] >= col_start[:, None]) & (qcol[None, :] < col_start[:, None] + NA_WIN_C)
    dc_idx = np.clip(qcol[None, :] - qcol[:, None], -(NA_WIN_C - 1), NA_WIN_C - 1) + NA_WIN_C - 1
    rpb_cols = rpb[:, :, dc_idx]
    out = []
    for _, valid, dr in _na_window_tables(img_rows):
        t = rpb_cols[:, dr]
        mask = valid[:, :, None, None] & col_mask[None, None]
        t = jnp.where(mask[None], t * math.log2(math.e), NEG_BIG)
        t = t.transpose(0, 2, 4, 1, 3).reshape(rpb.shape[0], NA_K_ROWS * GRID_W, NA_Q_ROWS * GRID_W)
        out.append(t)
    return jnp.stack(out, axis=1)


def _na_kernel(q_ref, k_ref, vt_ref, kc_ref, vct_ref, bias_ref, ot_ref, *, img_rows):
    qb = NA_Q_ROWS * GRID_W
    kb = NA_K_ROWS * GRID_W
    n_blk = img_rows // NA_Q_ROWS
    kc = kc_ref[...]
    vct = vct_ref[...]

    def offsets(i):
        if isinstance(i, int):
            return i * qb, min(max(i * NA_Q_ROWS - NA_WIN_R // 2, 0), img_rows - NA_K_ROWS) * GRID_W
        q0 = pl.multiple_of(i * qb, qb)
        return q0, pl.multiple_of(q0 - (NA_WIN_R // 2) * GRID_W, qb)

    def logits(i, variant):
        q0, ks = offsets(i)
        q = q_ref[pl.ds(q0, qb), :]
        return _dot_nt(k_ref[pl.ds(ks, kb), :], q) + bias_ref[variant], _dot_nt(kc, q)

    def softmax(s_w, s_c):
        m = jnp.maximum(jnp.max(s_w, axis=0, keepdims=True), jnp.max(s_c, axis=0, keepdims=True))
        p_w = jnp.exp2(s_w - m)
        p_c = jnp.exp2(s_c - m)
        l = jnp.sum(p_w, axis=0, keepdims=True) + jnp.sum(p_c, axis=0, keepdims=True)
        return p_w.astype(BF16), p_c.astype(BF16), l

    def weighted_values(i, p_w, p_c, l):
        q0, ks = offsets(i)
        o = _dot(vt_ref[:, pl.ds(ks, kb)], p_w) + _dot(vct, p_c)
        ot_ref[:, pl.ds(q0, qb)] = (o / l).astype(ot_ref.dtype)

    def group(blocks):
        n = len(blocks)
        s, p = {}, {}
        for step in range(n + 2):
            if step < n:
                s[step] = logits(*blocks[step])
            if 1 <= step <= n:
                p[step - 1] = softmax(*s.pop(step - 1))
            if 2 <= step:
                weighted_values(blocks[step - 2][0], *p.pop(step - 2))

    group([(0, 0)] + [(i, 1) for i in range(1, n_blk - 1)] + [(n_blk - 1, 2)])


def na_attention(q_lat, k_lat, vt_lat, k_ctx, vt_ctx, bias, *, batch, seq, lc):
    heads = NA_HEADS
    dh = q_lat.shape[1] // heads
    img_rows = seq // GRID_W
    qb = NA_Q_ROWS * GRID_W
    kb = NA_K_ROWS * GRID_W
    assert (NA_WIN_R // 2) * GRID_W == qb
    return pl.pallas_call(
        functools.partial(_na_kernel, img_rows=img_rows),
        grid=(batch, heads),
        in_specs=[
            pl.BlockSpec((seq, dh), lambda b, h: (b, h)),
            pl.BlockSpec((seq, dh), lambda b, h: (b, h)),
            pl.BlockSpec((dh, seq), lambda b, h: (h, b)),
            pl.BlockSpec((lc, dh), lambda b, h: (b, h)),
            pl.BlockSpec((dh, lc), lambda b, h: (h, b)),
            pl.BlockSpec((None, 3, kb, qb), lambda b, h: (h, 0, 0, 0)),
        ],
        out_specs=pl.BlockSpec((dh, seq), lambda b, h: (h, b)),
        out_shape=jax.ShapeDtypeStruct((heads * dh, batch * seq), BF16),
        compiler_params=_cparams("parallel", "parallel"),
        name="na_attention",
    )(q_lat, k_lat, vt_lat, k_ctx, vt_ctx, bias)


def _mla_up_kernel(z_ref, gq_ref, gkv_ref, wq_ref, wkv_ref, cos_ref, sin_ref,
                   q_ref, k_ref, vt_ref, *maybe_v_ref, scale):
    hq = MLA_HEADS * MLA_QK_PAD
    z = z_ref[...]
    cq = _rms(z[:, :MLA_Q_RANK], gq_ref[...]).astype(BF16)
    ckv = _rms(z[:, MLA_Q_RANK:MLA_Q_RANK + MLA_KV_RANK], gkv_ref[...]).astype(BF16)
    cos = cos_ref[...]
    sin = sin_ref[...]
    c0 = MLA_Q_RANK + MLA_KV_RANK
    kpe = (z[:, c0:c0 + LANE] * cos + z[:, c0 + LANE:c0 + 2 * LANE] * sin).astype(BF16)
    qraw = _dot(cq, wq_ref[...])
    kv = _dot(ckv, wkv_ref[...])
    for h in range(MLA_HEADS):
        a = h * MLA_QK_PAD
        q_ref[:, a:a + LANE] = (qraw[:, a:a + LANE] * scale).astype(BF16)
        pe = qraw[:, a + LANE:a + 2 * LANE]
        sw = qraw[:, hq + h * LANE:hq + (h + 1) * LANE]
        q_ref[:, a + LANE:a + 2 * LANE] = ((pe * cos + sw * sin) * scale).astype(BF16)
        k_ref[:, a:a + LANE] = kv[:, h * MLA_NOPE:(h + 1) * MLA_NOPE].astype(BF16)
        k_ref[:, a + LANE:a + 2 * LANE] = kpe
    v = kv[:, MLA_HEADS * MLA_NOPE:]
    vt_ref[...] = v.T.astype(BF16)
    for v_ref in maybe_v_ref:
        v_ref[...] = v.astype(BF16)


def mla_up(z, g_q, g_kv, wq, wkv, cos, sin, *, with_v, tm=256):
    rows = z.shape[0]
    tm = min(tm, rows)
    n_pos = cos.shape[0] // tm
    hq = MLA_HEADS * MLA_QK_PAD
    hv = MLA_HEADS * MLA_V
    v_specs = [pl.BlockSpec((tm, hv), lambda i: (i, 0))] if with_v else []
    v_shapes = [jax.ShapeDtypeStruct((rows, hv), BF16)] if with_v else []
    return pl.pallas_call(
        functools.partial(_mla_up_kernel, scale=(MLA_NOPE + MLA_ROPE) ** -0.5 * math.log2(math.e)),
        grid=(rows // tm,),
        in_specs=[
            pl.BlockSpec((tm, z.shape[1]), lambda i: (i, 0)),
            pl.BlockSpec((1, MLA_Q_RANK), lambda i: (0, 0)),
            pl.BlockSpec((1, MLA_KV_RANK), lambda i: (0, 0)),
            pl.BlockSpec(wq.shape, lambda i: (0, 0)),
            pl.BlockSpec(wkv.shape, lambda i: (0, 0)),
            pl.BlockSpec((tm, LANE), lambda i: (i % n_pos, 0)),
            pl.BlockSpec((tm, LANE), lambda i: (i % n_pos, 0)),
        ],
        out_specs=[
            pl.BlockSpec((tm, hq), lambda i: (i, 0)),
            pl.BlockSpec((tm, hq), lambda i: (i, 0)),
            pl.BlockSpec((hv, tm), lambda i: (0, i)),
        ] + v_specs,
        out_shape=[
            jax.ShapeDtypeStruct((rows, hq), BF16),
            jax.ShapeDtypeStruct((rows, hq), BF16),
            jax.ShapeDtypeStruct((hv, rows), BF16),
        ] + v_shapes,
        compiler_params=_cparams("parallel"),
        name="mla_up",
    )(z, g_q.reshape(1, -1), g_kv.reshape(1, -1), wq, wkv, cos, sin)


def _mla_attn_kernel(q_ref, k_ref, vt_ref, kc_ref, vct_ref, ot_ref, *, tk):
    q = q_ref[...]
    n_tiles = 1 + k_ref.shape[0] // tk

    def logits(j):
        k = kc_ref[...] if j == 0 else k_ref[(j - 1) * tk:j * tk, :]
        return _dot_nt(k, q)

    def values(j):
        return vct_ref[...] if j == 0 else vt_ref[:, (j - 1) * tk:j * tk]

    m = l = acc = pending = None
    st = {0: logits(0)}
    for step in range(n_tiles + 1):
        if step + 1 < n_tiles:
            st[step + 1] = logits(step + 1)
        if step < n_tiles:
            s = st.pop(step)
            m_tile = jnp.max(s, axis=0, keepdims=True)
            m_new = m_tile if m is None else jnp.maximum(m, m_tile)
            alpha = None if m is None else jnp.exp2(m - m_new)
            p = jnp.exp2(s - m_new)
            m = m_new
            done_softmax = (step, alpha, p.astype(BF16), jnp.sum(p, axis=0, keepdims=True))
        if pending is not None:
            j, alpha_j, p_j, p_sum = pending
            pv = _dot(values(j), p_j)
            l, acc = (p_sum, pv) if acc is None else (alpha_j * l + p_sum, alpha_j * acc + pv)
        pending = done_softmax if step < n_tiles else None
    ot_ref[...] = (acc / l).astype(ot_ref.dtype)


def mla_attention(q_lat, k_lat, vt_lat, k_ctx, vt_ctx, *, batch, seq, lc, tq=512, tk=1024):
    heads = MLA_HEADS
    tq = min(tq, seq)
    tk = min(tk, seq)
    nq = seq // tq
    return pl.pallas_call(
        functools.partial(_mla_attn_kernel, tk=tk),
        grid=(batch, heads, nq),
        in_specs=[
            pl.BlockSpec((tq, MLA_QK_PAD), lambda b, h, i: (b * nq + i, h)),
            pl.BlockSpec((seq, MLA_QK_PAD), lambda b, h, i: (b, h)),
            pl.BlockSpec((MLA_V, seq), lambda b, h, i: (h, b)),
            pl.BlockSpec((lc, MLA_QK_PAD), lambda b, h, i: (b, h)),
            pl.BlockSpec((MLA_V, lc), lambda b, h, i: (h, b)),
        ],
        out_specs=pl.BlockSpec((MLA_V, tq), lambda b, h, i: (h, b * nq + i)),
        out_shape=jax.ShapeDtypeStruct((heads * MLA_V, batch * seq), BF16),
        compiler_params=_cparams("parallel", "parallel", "arbitrary"),
        name="mla_attention",
    )(q_lat, k_lat, vt_lat, k_ctx, vt_ctx)


def _rope_tables(seq):
    t = np.arange(seq)
    pos = np.stack([t // GRID_W, t % GRID_W], axis=-1).astype(np.float32)
    freqs = jnp.asarray(ROPE_BASE, F32) ** (-jnp.arange(ROPE_PAIRS, dtype=F32) / ROPE_PAIRS)
    ang = jnp.asarray(pos)[:, :, None] * freqs
    cos, sin = jnp.cos(ang), jnp.sin(ang)
    cos_t = jnp.stack([cos, cos], axis=2).reshape(seq, MLA_ROPE)
    sin_t = jnp.stack([-sin, sin], axis=2).reshape(seq, MLA_ROPE)
    pad = jnp.zeros((seq, LANE - MLA_ROPE), F32)
    return jnp.concatenate([cos_t, pad], axis=1), jnp.concatenate([sin_t, pad], axis=1)


_ROPE_SWAP = np.arange(MLA_ROPE).reshape(2, 2, ROPE_PAIRS)[:, ::-1, :].reshape(-1)


def _mla_weights(w_in, w_uq, w_ukv):
    d = w_in.shape[0]
    c0 = MLA_Q_RANK + MLA_KV_RANK
    zpad = jnp.zeros((d, LANE - MLA_ROPE), w_in.dtype)
    kpe = w_in[:, c0:]
    w_in_x = jnp.concatenate([w_in[:, :c0], kpe, zpad, kpe[:, _ROPE_SWAP], zpad], axis=1)
    r = w_uq.shape[0]
    wq = w_uq.reshape(r, MLA_HEADS, MLA_NOPE + MLA_ROPE)
    zq = jnp.zeros((r, MLA_HEADS, LANE - MLA_ROPE), w_uq.dtype)
    main = jnp.concatenate([wq, zq], axis=2).reshape(r, MLA_HEADS * MLA_QK_PAD)
    swapped = jnp.concatenate([wq[:, :, MLA_NOPE:][:, :, _ROPE_SWAP], zq], axis=2).reshape(r, MLA_HEADS * LANE)
    wq_x = jnp.concatenate([main, swapped], axis=1)
    wkv = w_ukv.reshape(w_ukv.shape[0], MLA_HEADS, MLA_NOPE + MLA_V)
    wkv_x = jnp.concatenate([wkv[:, :, :MLA_NOPE].reshape(r, -1), wkv[:, :, MLA_NOPE:].reshape(r, -1)], axis=1)
    return w_in_x.astype(BF16), wq_x.astype(BF16), wkv_x.astype(BF16)


def _ml_conv_kernel(prev_ref, x_ref, next_ref, w_ref, q_ref, k_ref, *, seg_len, scale):
    tm = x_ref.shape[0]
    hk = ML_HEADS * ML_QK
    x = x_ref[...]
    row = lax.broadcasted_iota(jnp.int32, (tm, 1), 0)
    pos = (pl.program_id(0) * tm) % seg_len + row
    x_prev = jnp.where(row == 0, prev_ref[7:8, :], pltpu.roll(x, 1, 0))
    x_next = jnp.where(row == tm - 1, next_ref[0:1, :], pltpu.roll(x, tm - 1, 0))
    x_prev = jnp.where(pos == 0, 0.0, x_prev)
    x_next = jnp.where(pos == seg_len - 1, 0.0, x_next)
    y = x_prev * w_ref[0:1, :] + x * w_ref[1:2, :] + x_next * w_ref[2:3, :]
    y = y * jax.nn.sigmoid(y)
    q_ref[...] = (y[:, :hk] * scale).astype(BF16)
    k_ref[...] = y[:, hk:].astype(BF16)


def ml_conv_silu(qk_raw, conv_w, *, seg_len, tm=256):
    rows, c = qk_raw.shape
    tm = min(tm, seg_len)
    hk = ML_HEADS * ML_QK
    nb8 = rows // 8
    t8 = tm // 8
    return pl.pallas_call(
        functools.partial(_ml_conv_kernel, seg_len=seg_len, scale=ML_QK ** -0.5),
        grid=(rows // tm,),
        in_specs=[
            pl.BlockSpec((8, c), lambda i: (jnp.maximum(i * t8 - 1, 0), 0)),
            pl.BlockSpec((tm, c), lambda i: (i, 0)),
            pl.BlockSpec((8, c), lambda i: (jnp.minimum((i + 1) * t8, nb8 - 1), 0)),
            pl.BlockSpec((ML_CONV, c), lambda i: (0, 0)),
        ],
        out_specs=[pl.BlockSpec((tm, hk), lambda i: (i, 0)), pl.BlockSpec((tm, hk), lambda i: (i, 0))],
        out_shape=[jax.ShapeDtypeStruct((rows, hk), BF16), jax.ShapeDtypeStruct((rows, hk), BF16)],
        compiler_params=_cparams("parallel"),
        name="ml_conv_silu",
    )(qk_raw, qk_raw, qk_raw, conv_w)


def _log_sigmoid(x):
    return jnp.minimum(x, 0.0) - jnp.log(1.0 + jnp.exp(-jnp.abs(x)))


def _running_sum(x, axis, reverse):
    n = x.shape[axis]
    idx = lax.broadcasted_iota(jnp.int32, x.shape, axis)
    shift = 1
    while shift < n:
        if reverse:
            x = x + jnp.where(idx < n - shift, pltpu.roll(x, n - shift, axis), 0.0)
        else:
            x = x + jnp.where(idx >= shift, pltpu.roll(x, shift, axis), 0.0)
        shift *= 2
    return x


ML_STATE_ROWS = ML_V + 16


def _ml_scan_kernel(q_ref, k_ref, vt_ref, g_ref, gt_ref, b_ref, bt_ref, cn0_ref, m0_ref,
                    ht_ref, cn_out, m_out, cn_scr, m_scr, *, reverse):
    step = pl.program_id(1)
    tc = q_ref.shape[0]

    @pl.when(step == 0)
    def _():
        cn_scr[...] = cn0_ref[...]
        m_scr[...] = m0_ref[...]

    s_idx = lax.broadcasted_iota(jnp.int32, (tc, tc), 0)
    t_idx = lax.broadcasted_iota(jnp.int32, (tc, tc), 1)
    seen = (s_idx >= t_idx) if reverse else (s_idx <= t_idx)
    gi = 2 * ML_HEADS if reverse else 0
    gf = gi + ML_HEADS

    g = g_ref[...] + b_ref[...]
    gt = gt_ref[...] + bt_ref[...]
    lsg_t = _log_sigmoid(gt)
    cum_c = _running_sum(_log_sigmoid(g), 0, reverse)
    cum_r = _running_sum(lsg_t, 1, reverse)

    for hd in range(ML_HEADS):
        qh = q_ref[:, hd * ML_QK:(hd + 1) * ML_QK]
        kh = k_ref[:, hd * ML_QK:(hd + 1) * ML_QK]
        vth = vt_ref[hd * ML_V:(hd + 1) * ML_V, :]
        b_r = cum_r[gf + hd:gf + hd + 1, :]
        b_end = jnp.sum(lsg_t[gf + hd:gf + hd + 1, :], axis=1, keepdims=True)
        col = g[:, gi + hd:gi + hd + 1] - cum_c[:, gf + hd:gf + hd + 1]
        m_prev = m_scr[hd:hd + 1, 0:1]
        cn_prev = cn_scr[hd]

        log_w = jnp.where(seen, b_r + col, NEG_BIG)
        log_inter = b_r + m_prev
        m_t = jnp.maximum(log_inter, jnp.max(log_w, axis=0, keepdims=True))
        w_inter = jnp.exp(log_inter - m_t)
        st = _dot_nt(kh, qh) * jnp.exp(log_w - m_t)
        inter = _dot_nt(cn_prev.astype(BF16), qh)
        num = w_inter * inter[:ML_V, :] + _dot(vth, st.astype(BF16))
        den = w_inter * inter[ML_V:ML_V + 1, :] + jnp.sum(st, axis=0, keepdims=True)
        ht_ref[hd * ML_V:(hd + 1) * ML_V, :] = num / jnp.maximum(jnp.abs(den), jnp.exp(-m_t))

        log_s = col + b_end
        m_new = jnp.maximum(b_end + m_prev, jnp.max(log_s, axis=0, keepdims=True))
        decay = jnp.exp(b_end + m_prev - m_new)
        kw = kh.astype(F32) * jnp.exp(log_s - m_new)
        cn_scr[hd, :ML_V, :] = decay * cn_prev[:ML_V, :] + _dot(vth, kw.astype(BF16))
        cn_scr[hd, ML_V:ML_V + 1, :] = decay * cn_prev[ML_V:ML_V + 1, :] + jnp.sum(kw, axis=0, keepdims=True)
        m_scr[hd:hd + 1, :] = jnp.broadcast_to(m_new, (1, LANE))

    @pl.when(step == pl.num_programs(1) - 1)
    def _():
        cn_out[...] = cn_scr[...]
        m_out[...] = m_scr[...]


def ml_scan(q, k, vt, g, gt, b_row, b_col, state, *, batch, seg_len, reverse):
    tc = min(ML_CHUNK, seg_len)
    nc = seg_len // tc
    hk = ML_HEADS * ML_QK
    hv = ML_HEADS * ML_V
    cn0, m0 = state

    def blk(b, s):
        return b * nc + ((nc - 1 - s) if reverse else s)

    state_specs = [
        pl.BlockSpec((None, ML_HEADS, ML_STATE_ROWS, ML_QK), lambda b, s: (b, 0, 0, 0)),
        pl.BlockSpec((None, ML_HEADS, LANE), lambda b, s: (b, 0, 0)),
    ]
    return pl.pallas_call(
        functools.partial(_ml_scan_kernel, reverse=reverse),
        grid=(batch, nc),
        in_specs=[
            pl.BlockSpec((tc, hk), lambda b, s: (blk(b, s), 0)),
            pl.BlockSpec((tc, hk), lambda b, s: (blk(b, s), 0)),
            pl.BlockSpec((hv, tc), lambda b, s: (0, blk(b, s))),
            pl.BlockSpec((tc, LANE), lambda b, s: (blk(b, s), 0)),
            pl.BlockSpec((4 * ML_HEADS, tc), lambda b, s: (0, blk(b, s))),
            pl.BlockSpec((1, LANE), lambda b, s: (0, 0)),
            pl.BlockSpec((4 * ML_HEADS, 1), lambda b, s: (0, 0)),
        ] + state_specs,
        out_specs=[pl.BlockSpec((hv, tc), lambda b, s: (0, blk(b, s)))] + state_specs,
        out_shape=[
            jax.ShapeDtypeStruct((hv, batch * seg_len), F32),
            jax.ShapeDtypeStruct(cn0.shape, F32),
            jax.ShapeDtypeStruct(m0.shape, F32),
        ],
        scratch_shapes=[
            pltpu.VMEM((ML_HEADS, ML_STATE_ROWS, ML_QK), F32),
            pltpu.VMEM((ML_HEADS, LANE), F32),
        ],
        compiler_params=_cparams("parallel", "arbitrary"),
        name="ml_scan_bwd" if reverse else "ml_scan_fwd",
    )(q, k, vt, g, gt, b_row, b_col, cn0, m0)


def _ml_out_kernel(hf_ref, hb_ref, og_ref, gh_ref, w_ref, h_ref, g_ref, mod_ref, out_ref, x_scr, *, gate_idx):
    for hd in range(ML_HEADS):
        sl = slice(hd * ML_V, (hd + 1) * ML_V)
        hh = hf_ref[sl, :] + hb_ref[sl, :]
        hn = hh * lax.rsqrt(jnp.mean(hh * hh, axis=0, keepdims=True) + EPS) * gh_ref[sl, :]
        x_scr[sl, :] = (jax.nn.sigmoid(og_ref[sl, :]) * hn).astype(BF16)
    y = _dot_tn(x_scr[...], w_ref[...])
    out_ref[...] = h_ref[...] + mod_ref[gate_idx:gate_idx + 1, :] * _rms(y, g_ref[...])


def ml_out_residual(ht_f, ht_b, ogt, g_head, w, h, g, mod, *, gate_idx, rows_per_mod, mod_base, tm=256):
    hv, rows = ht_f.shape
    d = w.shape[1]
    tm = min(tm, rows)
    return pl.pallas_call(
        functools.partial(_ml_out_kernel, gate_idx=gate_idx),
        grid=(rows // tm,),
        in_specs=[
            pl.BlockSpec((hv, tm), lambda i: (0, i)),
            pl.BlockSpec((hv, tm), lambda i: (0, i)),
            pl.BlockSpec((hv, tm), lambda i: (0, i)),
            pl.BlockSpec((hv, 1), lambda i: (0, 0)),
            pl.BlockSpec((hv, d), lambda i: (0, 0)),
            pl.BlockSpec((tm, d), lambda i: (i, 0)),
            pl.BlockSpec((1, d), lambda i: (0, 0)),
            pl.BlockSpec((None, N_MOD, d), _mod_index_map(tm, rows_per_mod, mod_base)),
        ],
        out_specs=pl.BlockSpec((tm, d), lambda i: (i, 0)),
        out_shape=jax.ShapeDtypeStruct((rows, d), F32),
        scratch_shapes=[pltpu.VMEM((hv, tm), BF16)],
        compiler_params=_cparams("parallel"),
        name="ml_out_residual",
    )(ht_f, ht_b, ogt, g_head.reshape(hv, 1), w, h, g.reshape(1, d), mod)


def _streams(batch, seq, lc):
    return dict(rows_per_mod=seq, mod_base=0), dict(rows_per_mod=batch * lc, mod_base=batch)


def na_mixer(h_lat, h_ctx, mod, g_pre, g_post, w_qkv, rpb, w_o, *, batch, seq, lc, need_ctx):
    lat, ctx = _streams(batch, seq, lc)
    hd = w_o.shape[0]
    dh = hd // NA_HEADS
    w = w_qkv.astype(BF16)
    wo = w_o.astype(BF16)
    q_out = ProjOut(w, BF16, scale=dh ** -0.5 * math.log2(math.e))
    k_out = ProjOut(w, BF16, col0=hd)
    v_out = ProjOut(w, BF16, col0=2 * hd)
    vt_out = v_out._replace(transposed=True)
    proj = functools.partial(norm_proj, g=g_pre, mod=mod, shift_idx=0, scale_idx=1, tn=512, n=hd)
    q_lat, k_lat, vt_lat = proj(h_lat, outs=[q_out, k_out, vt_out], tm=1024, **lat)
    ctx_outs = [q_out, k_out, vt_out] + ([v_out] if need_ctx else [])
    q_ctx, k_ctx, vt_ctx, *v_ctx = proj(h_ctx, outs=ctx_outs, tm=512, **ctx)
    bias = _na_bias_tables(rpb, seq // GRID_W)
    ot_lat = na_attention(q_lat, k_lat, vt_lat, k_ctx, vt_ctx, bias, batch=batch, seq=seq, lc=lc)
    h_lat = out_proj_residual(ot_lat, wo, h_lat, g_post, mod, gate_idx=2, o_transposed=True, **lat)
    if need_ctx:
        o_ctx = ctx_attention(q_ctx, k_ctx, v_ctx[0], batch=batch, heads=NA_HEADS, lc=lc, dq=dh, dv=dh,
                              q_off=0, k_off=0, v_off=0, scale=1.0, base2=True)
        h_ctx = out_proj_residual(o_ctx, wo, h_ctx, g_post, mod, gate_idx=2, **ctx)
    return h_lat, h_ctx


def mla_mixer(h_lat, h_ctx, mod, g_pre, g_post, w_in, g_q, g_kv, w_uq, w_ukv, w_o, *, batch, seq, lc, need_ctx):
    lat, ctx = _streams(batch, seq, lc)
    w_in_x, wq_x, wkv_x = _mla_weights(w_in, w_uq, w_ukv)
    wo = w_o.astype(BF16)
    proj = functools.partial(norm_proj, g=g_pre, outs=[ProjOut(w_in_x, F32)], mod=mod, shift_idx=0, scale_idx=1,
                             tn=w_in_x.shape[1])
    z_lat, = proj(h_lat, tm=1024, **lat)
    z_ctx, = proj(h_ctx, tm=512, **ctx)
    cos_l, sin_l = _rope_tables(seq)
    ones = jnp.concatenate([jnp.ones((lc, MLA_ROPE), F32), jnp.zeros((lc, LANE - MLA_ROPE), F32)], axis=1)
    q_lat, k_lat, vt_lat = mla_up(z_lat, g_q, g_kv, wq_x, wkv_x, cos_l, sin_l, with_v=False)
    q_ctx, k_ctx, vt_ctx, v_ctx = mla_up(z_ctx, g_q, g_kv, wq_x, wkv_x, ones, jnp.zeros_like(ones), with_v=True)
    ot_lat = mla_attention(q_lat, k_lat, vt_lat, k_ctx, vt_ctx, batch=batch, seq=seq, lc=lc)
    h_lat = out_proj_residual(ot_lat, wo, h_lat, g_post, mod, gate_idx=2, o_transposed=True, **lat)
    if need_ctx:
        o_ctx = ctx_attention(q_ctx, k_ctx, v_ctx, batch=batch, heads=MLA_HEADS, lc=lc, dq=MLA_QK_PAD, dv=MLA_V,
                              q_off=0, k_off=0, v_off=0, scale=1.0, base2=True)
        h_ctx = out_proj_residual(o_ctx, wo, h_ctx, g_post, mod, gate_idx=2, **ctx)
    return h_lat, h_ctx


def mlstm_mixer(h_lat, h_ctx, mod, g_pre, g_post, w_in, b_gate, conv_w, g_head, w_o, *, batch, seq, lc, need_ctx):
    lat, ctx = _streams(batch, seq, lc)
    d = w_in.shape[0]
    hk2 = 2 * ML_HEADS * ML_QK
    hv = ML_HEADS * ML_V
    ng = 4 * ML_HEADS
    assert hk2 == hv
    w_b = w_in.astype(BF16)
    w_g = jnp.concatenate([w_in[:, hk2 + 2 * hv:], jnp.zeros((d, LANE - ng), w_in.dtype)], axis=1).astype(BF16)
    wo = w_o.astype(BF16)
    b_row = jnp.concatenate([b_gate, jnp.zeros((LANE - ng,), F32)]).reshape(1, LANE)
    b_col = b_gate.reshape(ng, 1)

    def project(h, stream, seg_len):
        outs = [ProjOut(w_b, F32), ProjOut(w_b, BF16, transposed=True, col0=hk2),
                ProjOut(w_b, F32, transposed=True, col0=hk2 + hv)]
        qk_raw, vt, ogt, g = norm_proj(h, g_pre, outs, n=hv, narrow=ProjOut(w_g, F32),
                                       mod=mod, shift_idx=0, scale_idx=1, tm=512, tn=512, **stream)
        q, k = ml_conv_silu(qk_raw, conv_w, seg_len=seg_len)
        return q, k, vt, ogt, g, g[:, :ng].T

    def bidir(q, k, vt, g, gt, st_f, st_b, seg_len):
        h_f, *st_f = ml_scan(q, k, vt, g, gt, b_row, b_col, st_f, batch=batch, seg_len=seg_len, reverse=False)
        h_b, *st_b = ml_scan(q, k, vt, g, gt, b_row, b_col, st_b, batch=batch, seg_len=seg_len, reverse=True)
        return h_f, h_b, st_f, st_b

    zero = (jnp.zeros((batch, ML_HEADS, ML_STATE_ROWS, ML_QK), F32), jnp.zeros((batch, ML_HEADS, LANE), F32))
    qc, kc, vc, ogc, gc, gtc = project(h_ctx, ctx, lc)
    ql, kl, vl, ogl, gl, gtl = project(h_lat, lat, seq)
    hc_f, hc_b, st_f, st_b = bidir(qc, kc, vc, gc, gtc, zero, zero, lc)
    hl_f, hl_b, _, _ = bidir(ql, kl, vl, gl, gtl, st_f, st_b, seq)
    h_lat = ml_out_residual(hl_f, hl_b, ogl, g_head, wo, h_lat, g_post, mod, gate_idx=2, **lat)
    if need_ctx:
        h_ctx = ml_out_residual(hc_f, hc_b, ogc, g_head, wo, h_ctx, g_post, mod, gate_idx=2, **ctx)
    return h_lat, h_ctx


def kernel(x, c, ctx, c_ctx, ada_w, ada_b, norm_g, ff_w1, ff_w2, na_w_qkv, na_rpb, na_w_o, mla_w_in, mla_g_q, mla_g_kv, mla_w_uq, mla_w_ukv, mla_w_o, ml_w_in, ml_b_gate, ml_conv, ml_g_head, ml_w_o):
    batch, seq, d = x.shape
    lc = ctx.shape[1]
    depth = ada_w.shape[0]
    lat, cst = _streams(batch, seq, lc)

    n_cond = 8 * ((batch + 1 + 7) // 8)
    c_all = jnp.concatenate([c, c_ctx[None, :], jnp.zeros((n_cond - batch - 1, d), F32)], axis=0)
    mods = ada_modulation(c_all, ada_w, ada_b).reshape(depth, n_cond, N_MOD, d)

    h_lat = x.reshape(batch * seq, d)
    h_ctx = ctx.reshape(batch * lc, d)
    for i in range(depth):
        last = i == depth - 1
        mod = mods[i]
        g_pre1, g_post1, g_pre2, g_post2 = norm_g[i]
        kind, j = i % 3, i // 3
        dims = dict(batch=batch, seq=seq, lc=lc, need_ctx=not last)
        if kind == 0:
            h_lat, h_ctx = na_mixer(h_lat, h_ctx, mod, g_pre1, g_post1, na_w_qkv[j], na_rpb[j], na_w_o[j], **dims)
        elif kind == 1:
            h_lat, h_ctx = mla_mixer(h_lat, h_ctx, mod, g_pre1, g_post1, mla_w_in[j], mla_g_q[j], mla_g_kv[j],
                                     mla_w_uq[j], mla_w_ukv[j], mla_w_o[j], **dims)
        else:
            h_lat, h_ctx = mlstm_mixer(h_lat, h_ctx, mod, g_pre1, g_post1, ml_w_in[j], ml_b_gate[j], ml_conv[j],
                                       ml_g_head[j], ml_w_o[j], **dims)
        h_lat = ffn_residual(h_lat, g_pre2, g_post2, mod, ff_w1, ff_w2, i, **lat)
        if not last:
            h_ctx = ffn_residual(h_ctx, g_pre2, g_post2, mod, ff_w1, ff_w2, i, **cst)
    return h_lat.reshape(batch, seq, d)
```
